```python
import jax, jax.numpy as jnp
from jax import lax
import numpy as np

D_MODEL = 1024
BATCH = 8
SEQ = 4096
DEPTH = 2
DEC_BATCH = 32
DEC_SEQ = 4
PAST_LEN = 16384
PAGE_SIZE = 128

HG_HEADS = 4
HG_DK = 128
HG_DV = 128
HG_WIDTH = HG_HEADS * HG_DV
HG_CHUNK = 64
MB_HEADS = 8
MB_HEAD_DIM = 64
MB_WIDTH = MB_HEADS * MB_HEAD_DIM
MB_BLOCK = 256
MB_TOPK = 3
MB_QBLOCK = 16
ROPE_THETA = 10000.0
POOL_WINDOWS = (2, 4, 8, 16)
POOL_GROUPS = 4
POOL_GROUP_DIM = 128
POOL_WIDTH = POOL_GROUPS * POOL_GROUP_DIM
POOL_HIST = 15
N_BRANCH = 3
BRANCH_WIDTH = 512
D_FF = 4 * D_MODEL
EPS = 1e-6
NEG = -1e30
F_FLOOR = 1e-30
IN_WIDTHS = (HG_HEADS * HG_DK, HG_HEADS * HG_DK, HG_WIDTH, HG_WIDTH, MB_WIDTH, MB_WIDTH, MB_WIDTH, POOL_WIDTH, N_BRANCH * D_MODEL)
D_IN = sum(IN_WIDTHS)
IN_SPLITS = [sum(IN_WIDTHS[:i + 1]) for i in range(len(IN_WIDTHS) - 1)]

kernel_name = 'hybrid_hgrn2_moba_pool_gated_decoder_step'


def rmsnorm(x, g):
    xf = x.astype(jnp.float32)
    y = xf * lax.rsqrt(jnp.mean(xf * xf, axis=-1, keepdims=True) + EPS) * g.astype(jnp.float32)
    return y.astype(x.dtype)


def rope(x, pos):
    half = x.shape[-1] // 2
    inv = ROPE_THETA ** (-jnp.arange(half, dtype=jnp.float32) / half)
    ang = pos.astype(jnp.float32)[:, None] * inv[None, :]
    cos = jnp.cos(ang)[None, :, None, :]
    sin = jnp.sin(ang)[None, :, None, :]
    xf = x.astype(jnp.float32)
    x1, x2 = xf[..., :half], xf[..., half:]
    return jnp.concatenate([x1 * cos - x2 * sin, x2 * cos + x1 * sin], axis=-1).astype(x.dtype)


def gla_chunked(q, k, v, log_f, S0):
    B, T, H, _ = q.shape
    C = min(HG_CHUNK, T)
    n = -(-T // C)
    pad = n * C - T

    def prep(a):
        a = jnp.pad(a, ((0, 0), (0, pad), (0, 0), (0, 0)))
        return a.reshape(B, n, C, H, a.shape[-1]).transpose(1, 0, 3, 2, 4)

    causal = jnp.tril(jnp.ones((C, C), dtype=bool))[:, :, None]

    def step(S, inp):
        qc, kc, vc, lc = inp
        b = jnp.cumsum(lc, axis=2)
        o_inter = jnp.einsum('bhtk,bhkv->bhtv', qc * jnp.exp(b), S)
        diff = b[:, :, :, None, :] - b[:, :, None, :, :]
        decay = jnp.where(causal, jnp.exp(jnp.where(causal, diff, 0.0)), 0.0)
        att = jnp.einsum('bhtk,bhsk,bhtsk->bhts', qc, kc, decay)
        o = o_inter + jnp.einsum('bhts,bhsv->bhtv', att, vc)
        b_last = b[:, :, -1]
        S = jnp.exp(b_last)[..., None] * S + jnp.einsum('bhsk,bhsv->bhkv', kc * jnp.exp(b_last[:, :, None] - b), vc)
        return S, o

    S, o = lax.scan(step, S0, (prep(q), prep(k), prep(v), prep(log_f)))
    o = o.transpose(1, 0, 3, 2, 4).reshape(B, n * C, H, v.shape[-1])[:, :T]
    return o, S


def hgrn2_branch(qr, fr, ir, gr, lb, S0, onorm):
    B, T, _ = qr.shape
    shp = (B, T, HG_HEADS, HG_DK)
    fr32 = fr.astype(jnp.float32)
    q = jax.nn.silu(qr.astype(jnp.float32)).reshape(shp)
    f = lb + (1.0 - lb) * jax.nn.sigmoid(fr32)
    log_f = jnp.log(jnp.maximum(f, F_FLOOR)).reshape(shp)
    k = ((1.0 - lb) * jax.nn.sigmoid(-fr32)).reshape(shp)
    v = ir.astype(jnp.float32).reshape(B, T, HG_HEADS, HG_DV)
    o, S = gla_chunked(q, k, v, log_f, S0.astype(jnp.float32))
    o = rmsnorm(o, onorm) * jax.nn.silu(gr.astype(jnp.float32)).reshape(B, T, HG_HEADS, HG_DV)
    return o.reshape(B, T, HG_WIDTH).astype(qr.dtype), S


def moba_attend(q, q_pos, k_all, v_all):
    B, H, T, hd = q.shape
    nblk = k_all.shape[2] // MB_BLOCK
    kb = k_all.reshape(B, H, nblk, MB_BLOCK, hd)
    vb = v_all.reshape(B, H, nblk, MB_BLOCK, hd)
    kmean = jnp.mean(kb.astype(jnp.float32), axis=3)
    kk = min(MB_TOPK, nblk)
    QB = min(MB_QBLOCK, T)
    ng = -(-T // QB)
    padq = ng * QB - T
    qg = jnp.pad(q, ((0, 0), (0, 0), (0, padq), (0, 0))).reshape(B, H, ng, QB, hd).transpose(2, 0, 1, 3, 4)
    pg = jnp.pad(q_pos, (0, padq), mode='edge').reshape(ng, QB)
    bi = jnp.arange(B)[:, None, None, None]
    hi = jnp.arange(H)[None, :, None, None]
    offs = jnp.arange(MB_BLOCK)
    scale = hd ** -0.5

    def one(args):
        qq, pp = args
        qf = qq.astype(jnp.float32)
        own = pp // MB_BLOCK
        gate = jnp.einsum('bhqd,bhnd->bhqn', qf, kmean)
        eligible = jnp.arange(nblk)[None, :] < own[:, None]
        gate = jnp.where(eligible, gate, NEG)
        _, top_idx = lax.top_k(gate, kk)
        top_ok = top_idx < own[None, None, :, None]
        own_b = jnp.broadcast_to(own[None, None, :, None], (B, H, QB, 1))
        blk = jnp.concatenate([top_idx, own_b], axis=-1)
        slot_ok = jnp.concatenate([top_ok, jnp.ones((B, H, QB, 1), dtype=bool)], axis=-1)
        kg = kb[bi, hi, blk].astype(jnp.float32)
        vg = vb[bi, hi, blk].astype(jnp.float32)
        kpos = blk[..., None] * MB_BLOCK + offs
        mask = slot_ok[..., None] & (kpos <= pp[:, None, None])
        s = jnp.einsum('bhqd,bhqnkd->bhqnk', qf, kg) * scale
        s = jnp.where(mask, s, NEG).reshape(B, H, QB, -1)
        p = jax.nn.softmax(s, axis=-1)
        return jnp.einsum('bhqm,bhqmd->bhqd', p, vg.reshape(B, H, QB, -1, hd))

    o = lax.map(one, (qg, pg))
    return o.transpose(1, 2, 0, 3, 4).reshape(B, H, ng * QB, hd)[:, :, :T]


def moba_branch(qr, kr, vr, pos, qn, kn, k_past, v_past):
    B, T, _ = qr.shape
    shp = (B, T, MB_HEADS, MB_HEAD_DIM)
    q = rope(rmsnorm(qr.reshape(shp), qn), pos).transpose(0, 2, 1, 3)
    k = rope(rmsnorm(kr.reshape(shp), kn), pos).transpose(0, 2, 1, 3)
    v = vr.reshape(shp).transpose(0, 2, 1, 3)
    L = k_past.shape[2] + T
    nblk = -(-L // MB_BLOCK)
    pad = jnp.zeros((B, MB_HEADS, nblk * MB_BLOCK - L, MB_HEAD_DIM), k.dtype)
    k_all = jnp.concatenate([k_past.astype(k.dtype), k, pad], axis=2)
    v_all = jnp.concatenate([v_past.astype(v.dtype), v, pad], axis=2)
    o = moba_attend(q, pos, k_all, v_all)
    return o.transpose(0, 2, 1, 3).reshape(B, T, MB_WIDTH).astype(qr.dtype), k, v


def pool_branch(u, hist, pos, pool_w, pool_scale):
    B, T, _ = u.shape
    ext = jnp.concatenate([hist.astype(u.dtype), u], axis=1)
    cs = jnp.cumsum(ext.astype(jnp.float32), axis=1)
    cs = jnp.concatenate([jnp.zeros((B, 1, POOL_WIDTH), jnp.float32), cs], axis=1)
    end = cs[:, POOL_HIST + 1:]
    means = []
    for g, w in enumerate(POOL_WINDOWS):
        sl = slice(g * POOL_GROUP_DIM, (g + 1) * POOL_GROUP_DIM)
        start = cs[:, POOL_HIST + 1 - w:POOL_HIST + 1 - w + T, sl]
        cnt = jnp.minimum(pos + 1, w).astype(jnp.float32)[None, :, None]
        means.append((end[..., sl] - start) / cnt)
    d = (jnp.concatenate(means, axis=-1) - u.astype(jnp.float32)).reshape(B, T, POOL_GROUPS, POOL_GROUP_DIM)
    y = jnp.einsum('btgc,gcd->btgd', d, pool_w.astype(jnp.float32)).reshape(B, T, POOL_WIDTH)
    y = y * pool_scale.astype(jnp.float32)
    return y.astype(u.dtype), ext[:, -POOL_HIST:]


def layer(x, pos, lb, S0, hist, k_past, v_past, norm_mix, w_in, hg_onorm, mb_qnorm, mb_knorm,
          pool_w, pool_scale, w_branch, w_out, norm_ffn, w_ff1, w_ff2):
    B, T, _ = x.shape
    h = rmsnorm(x, norm_mix)
    z = jnp.einsum('btd,de->bte', h, w_in)
    qa, fa, ia, ga, qb, kb, vb, uc, gz = jnp.split(z, IN_SPLITS, axis=-1)
    a_out, S_new = hgrn2_branch(qa, fa, ia, ga, lb, S0, hg_onorm)
    b_out, k_new, v_new = moba_branch(qb, kb, vb, pos, mb_qnorm, mb_knorm, k_past, v_past)
    c_out, hist_new = pool_branch(uc, hist, pos, pool_w, pool_scale)
    gates = jax.nn.sigmoid(gz.astype(jnp.float32)).reshape(B, T, N_BRANCH, D_MODEL)
    branches = jnp.stack([a_out, b_out, c_out], axis=2)
    proj = jnp.einsum('btnw,nwd->btnd', branches, w_branch).astype(jnp.float32)
    merged = jnp.sum(gates * proj, axis=2).astype(x.dtype)
    x = x + jnp.einsum('btd,de->bte', merged, w_out)
    h2 = rmsnorm(x, norm_ffn)
    x = x + jnp.einsum('btf,fd->btd', jnp.square(jax.nn.relu(jnp.einsum('btd,df->btf', h2, w_ff1))), w_ff2)
    return x, k_new, v_new, S_new, hist_new


def setup_inputs(seed: int = 0) -> dict:
    key = jax.random.key(seed)
    ks = jax.random.split(key, 24)
    n_pages = PAST_LEN // PAGE_SIZE
    n_used = DEC_BATCH * n_pages
    n_phys = (5 * n_used) // 4
    f32 = jnp.float32
    nrm = lambda k, s, sc: jax.random.normal(k, s, f32) * sc
    page_table = jax.random.permutation(ks[0], n_phys)[:n_used].reshape(DEC_BATCH, n_pages).astype(jnp.int32)
    return {
        'x_prompt': nrm(ks[1], (BATCH, SEQ, D_MODEL), 1.0),
        'x_sample': nrm(ks[2], (DEC_BATCH, DEC_SEQ, D_MODEL), 1.0),
        'cache_k': nrm(ks[3], (n_phys, DEPTH, MB_HEADS, PAGE_SIZE, MB_HEAD_DIM), 1.0),
        'cache_v': nrm(ks[4], (n_phys, DEPTH, MB_HEADS, PAGE_SIZE, MB_HEAD_DIM), 1.0),
        'state_hgrn': nrm(ks[5], (DEPTH, DEC_BATCH, HG_HEADS, HG_DK, HG_DV), 0.5),
        'state_pool': nrm(ks[6], (DEPTH, DEC_BATCH, POOL_HIST, POOL_WIDTH), 1.0),
        'page_table': page_table,
        'norm_mix': 1.0 + nrm(ks[7], (DEPTH, D_MODEL), 0.1),
        'w_in': nrm(ks[8], (DEPTH, D_MODEL, D_IN), D_MODEL ** -0.5),
        'hg_lb': nrm(ks[9], (DEPTH, HG_HEADS * HG_DK), 1.0),
        'hg_onorm': 1.0 + nrm(ks[10], (DEPTH, HG_DV), 0.1),
        'mb_qnorm': 1.0 + nrm(ks[11], (DEPTH, MB_HEAD_DIM), 0.1),
        'mb_knorm': 1.0 + nrm(ks[12], (DEPTH, MB_HEAD_DIM), 0.1),
        'pool_w': nrm(ks[13], (DEPTH, POOL_GROUPS, POOL_GROUP_DIM, POOL_GROUP_DIM), POOL_GROUP_DIM ** -0.5),
        'pool_scale': 1.0 + nrm(ks[14], (DEPTH, POOL_WIDTH), 0.1),
        'w_branch': nrm(ks[15], (DEPTH, N_BRANCH, BRANCH_WIDTH, D_MODEL), BRANCH_WIDTH ** -0.5),
        'w_out': nrm(ks[16], (DEPTH, D_MODEL, D_MODEL), D_MODEL ** -0.5),
        'norm_ffn': 1.0 + nrm(ks[17], (DEPTH, D_MODEL), 0.1),
        'w_ff1': nrm(ks[18], (DEPTH, D_MODEL, D_FF), D_MODEL ** -0.5),
        'w_ff2': nrm(ks[19], (DEPTH, D_FF, D_MODEL), D_FF ** -0.5),
    }


def reference(x_prompt, x_sample, cache_k, cache_v, state_hgrn, state_pool, page_table, norm_mix, w_in,
              hg_lb, hg_onorm, mb_qnorm, mb_knorm, pool_w, pool_scale, w_branch, w_out, norm_ffn, w_ff1, w_ff2):
    B, T, _ = x_prompt.shape
    DB, TS, _ = x_sample.shape
    n_pages = page_table.shape[1]
    past_len = n_pages * PAGE_SIZE
    pos_p = jnp.arange(T, dtype=jnp.int32)
    pos_s = past_len + jnp.arange(TS, dtype=jnp.int32)
    lb_soft = jax.nn.softmax(hg_lb.astype(jnp.float32), axis=0)
    lb_all = jnp.cumsum(lb_soft, axis=0) - lb_soft[0:1]
    S0_p = jnp.zeros((B, HG_HEADS, HG_DK, HG_DV), jnp.float32)
    hist_p = jnp.zeros((B, POOL_HIST, POOL_WIDTH), x_prompt.dtype)
    kv0_p = jnp.zeros((B, MB_HEADS, 0, MB_HEAD_DIM), x_prompt.dtype)
    xp, xs = x_prompt, x_sample
    kp_l, vp_l, ks_l, vs_l, Sp_l, Ss_l, hp_l, hs_l = [], [], [], [], [], [], [], []
    for l in range(DEPTH):
        wts = (norm_mix[l], w_in[l], hg_onorm[l], mb_qnorm[l], mb_knorm[l], pool_w[l], pool_scale[l],
               w_branch[l], w_out[l], norm_ffn[l], w_ff1[l], w_ff2[l])
        xp, kp, vp, Sp, hp = layer(xp, pos_p, lb_all[l], S0_p, hist_p, kv0_p, kv0_p, *wts)
        k_past = cache_k[page_table, l].transpose(0, 2, 1, 3, 4).reshape(DB, MB_HEADS, past_len, MB_HEAD_DIM)
        v_past = cache_v[page_table, l].transpose(0, 2, 1, 3, 4).reshape(DB, MB_HEADS, past_len, MB_HEAD_DIM)
        xs, kq, vq, Sq, hq = layer(xs, pos_s, lb_all[l], state_hgrn[l], state_pool[l], k_past, v_past, *wts)
        kp_l.append(kp); vp_l.append(vp); ks_l.append(kq); vs_l.append(vq)
        Sp_l.append(Sp.astype(x_prompt.dtype)); Ss_l.append(Sq.astype(x_sample.dtype))
        hp_l.append(hp); hs_l.append(hq)
    return (xp, xs, jnp.stack(kp_l, axis=1), jnp.stack(vp_l, axis=1), jnp.stack(ks_l, axis=1), jnp.stack(vs_l, axis=1),
            jnp.stack(Sp_l, axis=0), jnp.stack(Ss_l, axis=0), jnp.stack(hp_l, axis=0), jnp.stack(hs_l, axis=0))
```

```python
import functools

import jax
import jax.numpy as jnp
import numpy as np
from jax import lax
from jax.experimental import pallas as pl
from jax.experimental.pallas import tpu as pltpu

D_MODEL = 1024
PAGE_SIZE = 128
HG_HEADS = 4
HG_DK = 128
HG_DV = 128
HG_WIDTH = HG_HEADS * HG_DV
HG_CHUNK = 64
HG_SUB = 16
MB_HEADS = 8
MB_HEAD_DIM = 64
MB_WIDTH = MB_HEADS * MB_HEAD_DIM
MB_BLOCK = 256
MB_TOPK = 3
ROPE_THETA = 10000.0
POOL_WINDOWS = (2, 4, 8, 16)
POOL_GROUPS = 4
POOL_GROUP_DIM = 128
POOL_WIDTH = POOL_GROUPS * POOL_GROUP_DIM
POOL_HIST = 15
POOL_HALO = 16
N_BRANCH = 3
BRANCH_WIDTH = 512
D_FF = 4 * D_MODEL
EPS = 1e-6
NEG = -1e30
BELOW_NEG = -3e38
F_FLOOR = 1e-30
D_IN = 4 * HG_WIDTH + 3 * MB_WIDTH + POOL_WIDTH + N_BRANCH * D_MODEL
COL_QA, COL_FA, COL_IA, COL_GA, COL_QB, COL_KB, COL_VB, COL_UC = range(8)
COL_GATE_1024 = 4

SAMPLE_T_PAD = 8
V7X_VMEM_BYTES = 64 * 1024 * 1024
VMEM_LIMIT = 56 * 1024 * 1024

_BF = jnp.bfloat16
_F32 = jnp.float32


def _cparams(sem, vmem=VMEM_LIMIT):
    return pltpu.CompilerParams(dimension_semantics=sem, vmem_limit_bytes=vmem)


def _dot(a, b):
    return jnp.dot(a, b, preferred_element_type=_F32)


def _dot_nt(a, b, precision=None):
    return lax.dot_general(a, b, (((1,), (1,)), ((), ())), preferred_element_type=_F32, precision=precision)


def _dot_tn(a, b):
    return lax.dot_general(a, b, (((0,), (0,)), ((), ())), preferred_element_type=_F32)


def _sigmoid(x):
    return 1.0 / (1.0 + jnp.exp(-x))


def _inproj_kernel(x_ref, g_ref, w_ref, z_ref, h_ref):
    @pl.when(pl.program_id(1) == 0)
    def _():
        x = x_ref[...]
        ms = jnp.mean(x * x, axis=-1, keepdims=True)
        h_ref[...] = (x * lax.rsqrt(ms + EPS) * g_ref[...]).astype(_BF)

    z_ref[...] = _dot(h_ref[...], w_ref[...])


def _inproj(x, g, w_bf, tm, tn):
    n = x.shape[0]
    return pl.pallas_call(
        _inproj_kernel,
        grid=(n // tm, D_IN // tn),
        in_specs=[
            pl.BlockSpec((tm, D_MODEL), lambda i, j: (i, 0)),
            pl.BlockSpec((1, D_MODEL), lambda i, j: (0, 0)),
            pl.BlockSpec((D_MODEL, tn), lambda i, j: (0, j)),
        ],
        out_specs=pl.BlockSpec((tm, tn), lambda i, j: (i, j)),
        out_shape=jax.ShapeDtypeStruct((n, D_IN), _F32),
        scratch_shapes=[pltpu.VMEM((tm, D_MODEL), _BF)],
        compiler_params=_cparams(("parallel", "arbitrary")),
        name="inproj",
    )(x, g.reshape(1, D_MODEL), w_bf)


def _hgrn_kernel(q_ref, f_ref, i_ref, g_ref, lb_ref, on_ref, s0_ref, a_ref, s_out_ref,
                 st_scr, b_scr, k_scr, v_scr, *, chunk, sub, t_valid):
    ci = pl.program_id(1)

    @pl.when(ci == 0)
    def _():
        st_scr[...] = s0_ref[0]

    row = lax.broadcasted_iota(jnp.int32, (chunk, 1), 0)
    tril = (lax.broadcasted_iota(jnp.int32, (chunk, chunk), 0)
            >= lax.broadcasted_iota(jnp.int32, (chunk, chunk), 1)).astype(_F32)
    sub_row = lax.broadcasted_iota(jnp.int32, (sub, 1), 0)
    n_sub = chunk // sub

    for h in range(HG_HEADS):
        sl = slice(h * HG_DK, (h + 1) * HG_DK)
        qr = q_ref[:, sl]
        fr = f_ref[:, sl]
        v = i_ref[:, sl]
        gr = g_ref[:, sl]
        lb = lb_ref[:, sl]
        q = qr * _sigmoid(qr)
        fval = lb + (1.0 - lb) * _sigmoid(fr)
        logf = jnp.log(jnp.maximum(fval, F_FLOOR))
        k = (1.0 - lb) * _sigmoid(-fr)
        if t_valid < chunk:
            valid = row < t_valid
            logf = jnp.where(valid, logf, 0.0)
            k = jnp.where(valid, k, 0.0)
        b = lax.dot_general(tril, logf, (((1,), (0,)), ((), ())), preferred_element_type=_F32,
                            precision=lax.Precision.HIGHEST)
        b_scr[h] = b
        k_scr[h] = k
        v_scr[h] = v
        st = st_scr[h]
        v_bf = v.astype(_BF)
        o = _dot_nt((q * jnp.exp(b)).astype(_BF), st.astype(_BF))

        o_parts = []
        for si in range(n_sub):
            r0 = si * sub
            q_s = q[r0:r0 + sub]
            b_s = b[r0:r0 + sub]
            o_s = o[r0:r0 + sub]
            if si > 0:
                ref = b[r0 - 1:r0]
                q_dec = (q_s * jnp.exp(b_s - ref)).astype(_BF)
                k_dec = (k[:r0] * jnp.exp(ref - b[:r0])).astype(_BF)
                att = _dot_nt(q_dec, k_dec)
                o_s = o_s + _dot(att.astype(_BF), v_bf[:r0])

            def diag_body(s, acc, r0=r0, q_s=q_s, b_s=b_s, h=h):
                b_row = b_scr[h, pl.ds(r0 + s, 1), :]
                k_row = k_scr[h, pl.ds(r0 + s, 1), :]
                v_row = v_scr[h, pl.ds(r0 + s, 1), :]
                p = q_s * jnp.exp(jnp.minimum(b_s - b_row, 0.0)) * k_row
                a = jnp.sum(p, axis=-1, keepdims=True)
                a = jnp.where(sub_row >= s, a, 0.0)
                return acc + a * v_row

            o_s = lax.fori_loop(0, sub, diag_body, o_s)
            o_parts.append(o_s)
        o = o_parts[0] if n_sub == 1 else jnp.concatenate(o_parts, axis=0)

        b_last = b[chunk - 1:chunk]
        k_end = (k * jnp.exp(b_last - b)).astype(_BF)
        st_new = st * jnp.exp(b_last) + _dot_tn(v_bf, k_end)
        st_scr[h] = st_new

        ms = jnp.mean(o * o, axis=-1, keepdims=True)
        a_ref[:, sl] = o * lax.rsqrt(ms + EPS) * on_ref[...] * (gr * _sigmoid(gr))

    @pl.when(ci == pl.num_programs(1) - 1)
    def _():
        s_out_ref[0] = st_scr[...]


def _hgrn(z, lb, onorm, s0_t, n_batch, t_pad, chunk, sub, t_valid):
    n_chunks = t_pad // chunk
    col = lambda c: pl.BlockSpec((chunk, HG_WIDTH), lambda b, ci, c=c: (b * n_chunks + ci, c))
    state_spec = pl.BlockSpec((1, HG_HEADS, HG_DV, HG_DK), lambda b, ci: (b, 0, 0, 0))
    kern = functools.partial(_hgrn_kernel, chunk=chunk, sub=sub, t_valid=t_valid)
    return pl.pallas_call(
        kern,
        grid=(n_batch, n_chunks),
        in_specs=[col(COL_QA), col(COL_FA), col(COL_IA), col(COL_GA),
                  pl.BlockSpec((1, HG_WIDTH), lambda b, ci: (0, 0)),
                  pl.BlockSpec((1, HG_DV), lambda b, ci: (0, 0)),
                  state_spec],
        out_specs=[pl.BlockSpec((chunk, HG_WIDTH), lambda b, ci: (b * n_chunks + ci, 0)), state_spec],
        out_shape=[jax.ShapeDtypeStruct((n_batch * t_pad, HG_WIDTH), _F32),
                   jax.ShapeDtypeStruct((n_batch, HG_HEADS, HG_DV, HG_DK), _F32)],
        scratch_shapes=[pltpu.VMEM((HG_HEADS, HG_DV, HG_DK), _F32),
                        pltpu.VMEM((HG_HEADS, chunk, HG_DK), _F32),
                        pltpu.VMEM((HG_HEADS, chunk, HG_DK), _F32),
                        pltpu.VMEM((HG_HEADS, chunk, HG_DV), _F32)],
        compiler_params=_cparams(("parallel", "arbitrary")),
        name="hgrn",
    )(z, z, z, z, lb.reshape(1, HG_WIDTH), onorm.reshape(1, HG_DV), s0_t)


def _mb_prep_kernel(q_ref, k_ref, v_ref, cos_ref, sin_ref, qn_ref, kn_ref, seg_ref,
                    qo_ref, ko_ref, vo_ref):
    cos = cos_ref[...]
    sin = sin_ref[...]
    lane = lax.broadcasted_iota(jnp.int32, (1, 128), 1)
    first_half = (lane % MB_HEAD_DIM) < (MB_HEAD_DIM // 2)

    def norm_rope(x, w):
        sq = x * x
        hi = sq.astype(_BF)
        lo = (sq - hi.astype(_F32)).astype(_BF)
        ss = _dot(hi, seg_ref[...]) + _dot(lo, seg_ref[...])
        xn = x * lax.rsqrt(ss * (1.0 / MB_HEAD_DIM) + EPS) * w
        outs = []
        for g in range(MB_WIDTH // 128):
            xg = xn[:, g * 128:(g + 1) * 128]
            rot = jnp.where(first_half, pltpu.roll(xg, 128 - MB_HEAD_DIM // 2, 1),
                            pltpu.roll(xg, MB_HEAD_DIM // 2, 1))
            outs.append(xg * cos + rot * sin)
        return outs

    q_g = norm_rope(q_ref[...], qn_ref[...])
    k_g = norm_rope(k_ref[...], kn_ref[...])
    v = v_ref[...]
    for h in range(MB_HEADS):
        g, off = divmod(h * MB_HEAD_DIM, 128)
        qo_ref[0, h] = q_g[g][:, off:off + MB_HEAD_DIM]
        ko_ref[0, h] = k_g[g][:, off:off + MB_HEAD_DIM]
        vo_ref[0, h] = v[:, h * MB_HEAD_DIM:(h + 1) * MB_HEAD_DIM]


def _mb_prep(z, cos_t, sin_t, qn, kn, seg, n_batch, t_pad, tm):
    n_t = t_pad // tm
    col = lambda c: pl.BlockSpec((tm, MB_WIDTH), lambda b, ti, c=c: (b * n_t + ti, c))
    tab = pl.BlockSpec((tm, 128), lambda b, ti: (ti, 0))
    vec = pl.BlockSpec((1, MB_WIDTH), lambda b, ti: (0, 0))
    out = pl.BlockSpec((1, MB_HEADS, tm, MB_HEAD_DIM), lambda b, ti: (b, 0, ti, 0))
    shp = jax.ShapeDtypeStruct((n_batch, MB_HEADS, t_pad, MB_HEAD_DIM), _F32)
    return pl.pallas_call(
        _mb_prep_kernel,
        grid=(n_batch, n_t),
        in_specs=[col(COL_QB), col(COL_KB), col(COL_VB), tab, tab, vec, vec,
                  pl.BlockSpec((MB_WIDTH, MB_WIDTH), lambda b, ti: (0, 0))],
        out_specs=[out, out, out],
        out_shape=[shp, shp, shp],
        compiler_params=_cparams(("parallel", "parallel")),
        name="mb_prep",
    )(z, z, z, cos_t, sin_t, jnp.tile(qn, MB_HEADS).reshape(1, MB_WIDTH),
      jnp.tile(kn, MB_HEADS).reshape(1, MB_WIDTH), seg)


def _top3_select(gate, n_cols):
    col = lax.broadcasted_iota(jnp.int32, gate.shape, 1).astype(_F32)
    sel = jnp.zeros(gate.shape, _F32)
    picks = []
    g = gate
    for _ in range(MB_TOPK):
        m = jnp.max(g, axis=1, keepdims=True)
        first = jnp.min(jnp.where(g == m, col, float(n_cols)), axis=1, keepdims=True)
        pick = col == first
        sel = jnp.where(pick, 1.0, sel)
        picks.append(first)
        g = jnp.where(pick, BELOW_NEG, g)
    return sel, picks


def _mb_attn_kernel(q_ref, k_ref, v_ref, o_ref, kbf_scr, vbf_scr, kmean_scr, *, n_blk, heads_per_step):
    i = pl.program_id(2)

    @pl.when(i == 0)
    def _():
        for h in range(heads_per_step):
            kbf_scr[h] = k_ref[0, h].astype(_BF)
            vbf_scr[h] = v_ref[0, h].astype(_BF)
            for j in range(n_blk):
                blk = k_ref[0, h, j * MB_BLOCK:(j + 1) * MB_BLOCK, :]
                kmean_scr[h, j:j + 1, :] = jnp.sum(blk, axis=0, keepdims=True) * (1.0 / MB_BLOCK)

    q_minus_k = (lax.broadcasted_iota(jnp.int32, (MB_BLOCK, MB_BLOCK), 0)
                 - lax.broadcasted_iota(jnp.int32, (MB_BLOCK, MB_BLOCK), 1))
    col = lax.broadcasted_iota(jnp.int32, (MB_BLOCK, n_blk), 1)
    outs = []
    for h in range(heads_per_step):
        qf = q_ref[0, h]
        gate = _dot_nt(qf, kmean_scr[h], precision=lax.Precision.HIGHEST)
        eligible = col < i
        gate = jnp.where(eligible, gate, NEG)
        sel, _ = _top3_select(gate, n_blk)
        sel_f = jnp.where(eligible, sel, 0.0)
        q_bf = (qf * (MB_HEAD_DIM ** -0.5)).astype(_BF)

        def kv_body(j, carry, h=h, q_bf=q_bf, sel_f=sel_f):
            m, l, acc = carry
            start = pl.multiple_of(j * MB_BLOCK, MB_BLOCK)
            kb = kbf_scr[h, pl.ds(start, MB_BLOCK), :]
            vb = vbf_scr[h, pl.ds(start, MB_BLOCK), :]
            s = _dot_nt(q_bf, kb)
            own = j == i
            picked = jnp.sum(jnp.where(col == j, sel_f, 0.0), axis=1, keepdims=True) + jnp.where(own, 1.0, 0.0)
            row_bias = (picked - 1.0) * (-NEG)
            slack = jnp.where(own, 0, MB_BLOCK)
            s = jnp.where(q_minus_k + slack >= 0, s + row_bias, NEG)
            m_new = jnp.maximum(m, jnp.max(s, axis=1, keepdims=True))
            alpha = jnp.exp(m - m_new)
            p = jnp.exp(s - m_new)
            l = alpha * l + jnp.sum(p, axis=1, keepdims=True)
            acc = alpha * acc + _dot(p.astype(_BF), vb)
            return m_new, l, acc

        init = (jnp.full((MB_BLOCK, 1), NEG, _F32), jnp.zeros((MB_BLOCK, 1), _F32),
                jnp.zeros((MB_BLOCK, MB_HEAD_DIM), _F32))
        _, l, acc = lax.fori_loop(0, i + 1, kv_body, init)
        outs.append(acc / l)
    o_ref[...] = jnp.concatenate(outs, axis=1)


def _mb_attn(q, k, v, n_batch, t):
    heads_per_step = 128 // MB_HEAD_DIM
    n_blk = t // MB_BLOCK
    n_hp = MB_HEADS // heads_per_step
    kern = functools.partial(_mb_attn_kernel, n_blk=n_blk, heads_per_step=heads_per_step)
    kv_spec = pl.BlockSpec((1, heads_per_step, t, MB_HEAD_DIM), lambda b, hp, i: (b, hp, 0, 0))
    return pl.pallas_call(
        kern,
        grid=(n_batch, n_hp, n_blk),
        in_specs=[pl.BlockSpec((1, heads_per_step, MB_BLOCK, MB_HEAD_DIM), lambda b, hp, i: (b, hp, i, 0)),
                  kv_spec, kv_spec],
        out_specs=pl.BlockSpec((MB_BLOCK, 128), lambda b, hp, i: (b * n_blk + i, hp)),
        out_shape=jax.ShapeDtypeStruct((n_batch * t, MB_WIDTH), _F32),
        scratch_shapes=[pltpu.VMEM((heads_per_step, t, MB_HEAD_DIM), _BF),
                        pltpu.VMEM((heads_per_step, t, MB_HEAD_DIM), _BF),
                        pltpu.VMEM((heads_per_step, n_blk, MB_HEAD_DIM), _F32)],
        compiler_params=_cparams(("parallel", "parallel", "arbitrary")),
        name="mb_attn",
    )(q, k, v)


PAGES_PER_BLOCK = MB_BLOCK // PAGE_SIZE
KMEAN_PAGES_PER_STEP = 16


def _kmean_kernel(pt_ref, *refs):
    page_refs, out_ref = refs[:KMEAN_PAGES_PER_STEP], refs[KMEAN_PAGES_PER_STEP]
    for jb in range(KMEAN_PAGES_PER_STEP // PAGES_PER_BLOCK):
        tot = None
        for p in range(PAGES_PER_BLOCK):
            s = jnp.sum(page_refs[jb * PAGES_PER_BLOCK + p][0, 0], axis=1)
            tot = s if tot is None else tot + s
        out_ref[0, 0, jb] = tot * (1.0 / MB_BLOCK)


def _kmean(cache_k, pt_flat, n_layers, n_batch, n_pages):
    n_steps = n_pages // KMEAN_PAGES_PER_STEP
    blocks_per_step = KMEAN_PAGES_PER_STEP // PAGES_PER_BLOCK

    def page_spec(p):
        return pl.BlockSpec((1, 1, MB_HEADS, PAGE_SIZE, MB_HEAD_DIM),
                            lambda l, b, c, pt, p=p: (pt[b * n_pages + c * KMEAN_PAGES_PER_STEP + p], l, 0, 0, 0))

    grid_spec = pltpu.PrefetchScalarGridSpec(
        num_scalar_prefetch=1,
        grid=(n_layers, n_batch, n_steps),
        in_specs=[page_spec(p) for p in range(KMEAN_PAGES_PER_STEP)],
        out_specs=pl.BlockSpec((1, 1, blocks_per_step, MB_HEADS, MB_HEAD_DIM), lambda l, b, c, pt: (l, b, c, 0, 0)),
    )
    return pl.pallas_call(
        _kmean_kernel,
        grid_spec=grid_spec,
        out_shape=jax.ShapeDtypeStruct((n_layers, n_batch, n_pages // PAGES_PER_BLOCK, MB_HEADS, MB_HEAD_DIM), _F32),
        compiler_params=_cparams(("parallel", "parallel", "parallel")),
        name="kmean",
    )(pt_flat, *([cache_k] * KMEAN_PAGES_PER_STEP))


def _gate_kernel(q_ref, km_ref, idx_ref, *, own):
    n_past = km_ref.shape[2]
    col = lax.broadcasted_iota(jnp.int32, (SAMPLE_T_PAD, n_past), 1)
    lane = lax.broadcasted_iota(jnp.int32, (SAMPLE_T_PAD, 128), 1)
    for h in range(MB_HEADS):
        gate = _dot_nt(q_ref[0, h], km_ref[0, h], precision=lax.Precision.HIGHEST)
        gate = jnp.where(col < own, gate, NEG)
        _, picks = _top3_select(gate, n_past)
        out = jnp.zeros((SAMPLE_T_PAD, 128), jnp.int32)
        for r, first in enumerate(picks):
            out = jnp.where(lane == r, first.astype(jnp.int32), out)
        idx_ref[0, h] = out


def _gate(q_s, kmean_l, own):
    n_batch = q_s.shape[0]
    n_past = kmean_l.shape[2]
    return pl.pallas_call(
        functools.partial(_gate_kernel, own=own),
        grid=(n_batch,),
        in_specs=[pl.BlockSpec((1, MB_HEADS, SAMPLE_T_PAD, MB_HEAD_DIM), lambda b: (b, 0, 0, 0)),
                  pl.BlockSpec((1, MB_HEADS, n_past, MB_HEAD_DIM), lambda b: (b, 0, 0, 0))],
        out_specs=pl.BlockSpec((1, MB_HEADS, SAMPLE_T_PAD, 128), lambda b: (b, 0, 0, 0)),
        out_shape=jax.ShapeDtypeStruct((n_batch, MB_HEADS, SAMPLE_T_PAD, 128), jnp.int32),
        compiler_params=_cparams(("parallel",)),
        name="mb_gate",
    )(q_s, kmean_l)


def _sample_attn_kernel(pt_ref, idx_ref, q_ref, kn_ref, vn_ref, *refs, t_s, own):
    n_slots = t_s * MB_TOPK * PAGES_PER_BLOCK
    k_pages, v_pages, o_ref = refs[:n_slots], refs[n_slots:2 * n_slots], refs[2 * n_slots]
    b = pl.program_id(0)
    h = pl.program_id(1)
    q = q_ref[0, 0] * (MB_HEAD_DIM ** -0.5)
    q_bf = q.astype(_BF)
    row = lax.broadcasted_iota(jnp.int32, (SAMPLE_T_PAD, 1), 0)

    scores = []
    for t in range(t_s):
        for j in range(MB_TOPK):
            blk = idx_ref[((b * MB_HEADS + h) * t_s + t) * MB_TOPK + j]
            allowed = row == jnp.where(blk < own, t, -1)
            for p in range(PAGES_PER_BLOCK):
                slot = (t * MB_TOPK + j) * PAGES_PER_BLOCK + p
                s = _dot_nt(q_bf, k_pages[slot][0, 0, 0].astype(_BF))
                scores.append(jnp.where(allowed, s, NEG))
    kn = kn_ref[0, 0]
    vn = vn_ref[0, 0]
    own_scores = []
    for c in range(t_s):
        s = jnp.sum(q * kn[c:c + 1, :], axis=1, keepdims=True)
        own_scores.append(jnp.where(row >= c, s, NEG))

    m = own_scores[0]
    for s in own_scores[1:]:
        m = jnp.maximum(m, s)
    for s in scores:
        m = jnp.maximum(m, jnp.max(s, axis=1, keepdims=True))
    l = jnp.zeros((SAMPLE_T_PAD, 1), _F32)
    acc = jnp.zeros((SAMPLE_T_PAD, MB_HEAD_DIM), _F32)
    for c, s in enumerate(own_scores):
        p = jnp.exp(s - m)
        l = l + p
        acc = acc + p * vn[c:c + 1, :]
    for slot, s in enumerate(scores):
        p = jnp.exp(s - m)
        l = l + jnp.sum(p, axis=1, keepdims=True)
        acc = acc + _dot(p.astype(_BF), v_pages[slot][0, 0, 0].astype(_BF))
    o_ref[0, 0] = acc / l


def _sample_attn(q_s, k_s, v_s, cache_k, cache_v, pt_flat, idx_flat, layer, n_pages, t_s, own):
    n_batch = q_s.shape[0]
    new_spec = pl.BlockSpec((1, 1, SAMPLE_T_PAD, MB_HEAD_DIM), lambda b, h, pt, idx: (b, h, 0, 0))

    def page_spec(t, j, p):
        def index_map(b, h, pt, idx):
            blk = idx[((b * MB_HEADS + h) * t_s + t) * MB_TOPK + j]
            return (pt[b * n_pages + blk * PAGES_PER_BLOCK + p], layer, h, 0, 0)
        return pl.BlockSpec((1, 1, 1, PAGE_SIZE, MB_HEAD_DIM), index_map)

    page_specs = [page_spec(t, j, p) for t in range(t_s) for j in range(MB_TOPK) for p in range(PAGES_PER_BLOCK)]
    n_slots = len(page_specs)
    grid_spec = pltpu.PrefetchScalarGridSpec(
        num_scalar_prefetch=2,
        grid=(n_batch, MB_HEADS),
        in_specs=[new_spec, new_spec, new_spec] + page_specs + page_specs,
        out_specs=new_spec,
    )
    return pl.pallas_call(
        functools.partial(_sample_attn_kernel, t_s=t_s, own=own),
        grid_spec=grid_spec,
        out_shape=jax.ShapeDtypeStruct((n_batch, MB_HEADS, SAMPLE_T_PAD, MB_HEAD_DIM), _F32),
        compiler_params=_cparams(("parallel", "parallel")),
        name="mb_sample_attn",
    )(pt_flat, idx_flat, q_s, k_s, v_s, *([cache_k] * n_slots), *([cache_v] * n_slots))


def _pool_kernel(u_ref, hist_ref, w_ref, sc_ref, y_ref, hist_out_ref, ext_scr, *, tm, pos0, t_last):
    ti = pl.program_id(1)

    @pl.when(ti == 0)
    def _():
        ext_scr[0:POOL_HALO, :] = hist_ref[0]

    u = u_ref[...]
    ext_scr[POOL_HALO:POOL_HALO + tm, :] = u
    pos = pos0 + ti * tm + lax.broadcasted_iota(jnp.int32, (tm, 1), 0)
    for g, w in enumerate(POOL_WINDOWS):
        sl = slice(g * POOL_GROUP_DIM, (g + 1) * POOL_GROUP_DIM)
        tot = u[:, sl]
        for d in range(1, w):
            tot = tot + ext_scr[POOL_HALO - d:POOL_HALO - d + tm, sl]
        cnt = jnp.minimum(pos + 1, w).astype(_F32)
        diff = tot / cnt - u[:, sl]
        y = _dot(diff.astype(_BF), w_ref[g])
        y_ref[:, sl] = y * sc_ref[:, sl]

    @pl.when(ti == pl.num_programs(1) - 1)
    def _():
        hist_out_ref[0] = ext_scr[t_last:t_last + POOL_HALO, :]

    @pl.when(ti < pl.num_programs(1) - 1)
    def _():
        ext_scr[0:POOL_HALO, :] = ext_scr[tm:tm + POOL_HALO, :]


def _pool(z, hist16, pool_w_bf, pool_scale, n_batch, t_pad, tm, pos0, t_valid):
    n_t = t_pad // tm
    t_last = t_valid - (n_t - 1) * tm
    hist_spec = pl.BlockSpec((1, POOL_HALO, POOL_WIDTH), lambda b, ti: (b, 0, 0))
    kern = functools.partial(_pool_kernel, tm=tm, pos0=pos0, t_last=t_last)
    return pl.pallas_call(
        kern,
        grid=(n_batch, n_t),
        in_specs=[pl.BlockSpec((tm, POOL_WIDTH), lambda b, ti: (b * n_t + ti, COL_UC)),
                  hist_spec,
                  pl.BlockSpec((POOL_GROUPS, POOL_GROUP_DIM, POOL_GROUP_DIM), lambda b, ti: (0, 0, 0)),
                  pl.BlockSpec((1, POOL_WIDTH), lambda b, ti: (0, 0))],
        out_specs=[pl.BlockSpec((tm, POOL_WIDTH), lambda b, ti: (b * n_t + ti, 0)), hist_spec],
        out_shape=[jax.ShapeDtypeStruct((n_batch * t_pad, POOL_WIDTH), _F32),
                   jax.ShapeDtypeStruct((n_batch, POOL_HALO, POOL_WIDTH), _F32)],
        scratch_shapes=[pltpu.VMEM((POOL_HALO + tm, POOL_WIDTH), _F32)],
        compiler_params=_cparams(("parallel", "arbitrary")),
        name="pool",
    )(z, hist16, pool_w_bf, pool_scale.reshape(1, POOL_WIDTH))


def _merge_kernel(x_ref, a_ref, b_ref, c_ref, g0_ref, g1_ref, g2_ref, wb_ref, wo_ref, o_ref):
    merged = None
    for br_ref, g_ref, n in ((a_ref, g0_ref, 0), (b_ref, g1_ref, 1), (c_ref, g2_ref, 2)):
        proj = _dot(br_ref[...].astype(_BF), wb_ref[n])
        term = _sigmoid(g_ref[...]) * proj
        merged = term if merged is None else merged + term
    o_ref[...] = x_ref[...] + _dot(merged.astype(_BF), wo_ref[...])


def _merge(x, a, b, c, z, wb_bf, wo_bf, tm):
    n = x.shape[0]
    row = lambda w: pl.BlockSpec((tm, w), lambda i: (i, 0))
    gate = lambda k: pl.BlockSpec((tm, D_MODEL), lambda i, k=k: (i, COL_GATE_1024 + k))
    return pl.pallas_call(
        _merge_kernel,
        grid=(n // tm,),
        in_specs=[row(D_MODEL), row(BRANCH_WIDTH), row(BRANCH_WIDTH), row(BRANCH_WIDTH),
                  gate(0), gate(1), gate(2),
                  pl.BlockSpec((N_BRANCH, BRANCH_WIDTH, D_MODEL), lambda i: (0, 0, 0)),
                  pl.BlockSpec((D_MODEL, D_MODEL), lambda i: (0, 0))],
        out_specs=row(D_MODEL),
        out_shape=jax.ShapeDtypeStruct((n, D_MODEL), _F32),
        compiler_params=_cparams(("parallel",)),
        name="merge",
    )(x, a, b, c, z, z, z, wb_bf, wo_bf)


def _ffn_kernel(x_ref, g_ref, w1_ref, w2_ref, o_ref):
    x = x_ref[...]
    ms = jnp.mean(x * x, axis=-1, keepdims=True)
    h = (x * lax.rsqrt(ms + EPS) * g_ref[...]).astype(_BF)
    hid = jnp.maximum(_dot(h, w1_ref[...]), 0.0)
    hid = (hid * hid).astype(_BF)
    o_ref[...] = x + _dot(hid, w2_ref[...])


def _ffn(x, g, w1_bf, w2_bf, tm):
    n = x.shape[0]
    row = pl.BlockSpec((tm, D_MODEL), lambda i: (i, 0))
    return pl.pallas_call(
        _ffn_kernel,
        grid=(n // tm,),
        in_specs=[row, pl.BlockSpec((1, D_MODEL), lambda i: (0, 0)),
                  pl.BlockSpec((D_MODEL, D_FF), lambda i: (0, 0)),
                  pl.BlockSpec((D_FF, D_MODEL), lambda i: (0, 0))],
        out_specs=row,
        out_shape=jax.ShapeDtypeStruct((n, D_MODEL), _F32),
        compiler_params=_cparams(("parallel",)),
        name="ffn",
    )(x, g.reshape(1, D_MODEL), w1_bf, w2_bf)


def _rope_tables(pos):
    half = MB_HEAD_DIM // 2
    inv = ROPE_THETA ** (-jnp.arange(half, dtype=_F32) / half)
    ang = pos.astype(_F32)[:, None] * inv[None, :]
    cos, sin = jnp.cos(ang), jnp.sin(ang)
    cos_t = jnp.concatenate([cos, cos, cos, cos], axis=1)
    sin_t = jnp.concatenate([-sin, sin, -sin, sin], axis=1)
    return cos_t, sin_t


def _tile_rows(n, cap):
    t = cap
    while n % t:
        t //= 2
    return t


def _layer(x, *, n_batch, t_pad, t_valid, pos0, chunk, sub, lb, s0_t, hist16, cos_t, sin_t, seg,
           attend, w):
    n = x.shape[0]
    z = _inproj(x, w["norm_mix"], w["w_in"], _tile_rows(n, 1024), D_IN // 4)
    a_out, s_new_t = _hgrn(z, lb, w["hg_onorm"], s0_t, n_batch, t_pad, chunk, sub, t_valid)
    q, k, v = _mb_prep(z, cos_t, sin_t, w["mb_qnorm"], w["mb_knorm"], seg, n_batch, t_pad, _tile_rows(t_pad, 512))
    b_out = attend(q, k, v)
    c_out, hist_new = _pool(z, hist16, w["pool_w"], w["pool_scale"], n_batch, t_pad, _tile_rows(t_pad, 512),
                            pos0, t_valid)
    x1 = _merge(x, a_out, b_out, c_out, z, w["w_branch"], w["w_out"], _tile_rows(n, 512))
    x2 = _ffn(x1, w["norm_ffn"], w["w_ff1"], w["w_ff2"], _tile_rows(n, 256))
    return x2, k, v, s_new_t, hist_new


def kernel(x_prompt, x_sample, cache_k, cache_v, state_hgrn, state_pool, page_table, norm_mix, w_in, hg_lb,
           hg_onorm, mb_qnorm, mb_knorm, pool_w, pool_scale, w_branch, w_out, norm_ffn, w_ff1, w_ff2):
    n_b, t_p, _ = x_prompt.shape
    n_db, t_s, _ = x_sample.shape
    depth = w_in.shape[0]
    n_pages = page_table.shape[1]
    past_len = n_pages * PAGE_SIZE
    assert t_p % MB_BLOCK == 0 and t_p % HG_CHUNK == 0
    assert past_len % MB_BLOCK == 0 and t_s <= SAMPLE_T_PAD and n_pages % KMEAN_PAGES_PER_STEP == 0
    own_s = past_len // MB_BLOCK

    lb_soft = jax.nn.softmax(hg_lb.astype(_F32), axis=0)
    lb_all = jnp.cumsum(lb_soft, axis=0) - lb_soft[0:1]

    seg = (jnp.arange(MB_WIDTH)[:, None] // MB_HEAD_DIM == jnp.arange(MB_WIDTH)[None, :] // MB_HEAD_DIM).astype(_BF)
    cos_p, sin_p = _rope_tables(jnp.arange(t_p, dtype=jnp.int32))
    cos_s, sin_s = _rope_tables(past_len + jnp.arange(SAMPLE_T_PAD, dtype=jnp.int32))

    pt_flat = page_table.reshape(-1).astype(jnp.int32)
    kmean_all = _kmean(cache_k, pt_flat, depth, n_db, n_pages)

    xp = x_prompt.reshape(n_b * t_p, D_MODEL)
    xs = jnp.pad(x_sample, ((0, 0), (0, SAMPLE_T_PAD - t_s), (0, 0))).reshape(n_db * SAMPLE_T_PAD, D_MODEL)

    s0_p = jnp.zeros((n_b, HG_HEADS, HG_DV, HG_DK), _F32)
    hist_p = jnp.zeros((n_b, POOL_HALO, POOL_WIDTH), _F32)

    kp_l, vp_l, ks_l, vs_l, sp_l, ss_l, hp_l, hs_l = [], [], [], [], [], [], [], []
    for l in range(depth):
        w = dict(norm_mix=norm_mix[l], w_in=w_in[l].astype(_BF), hg_onorm=hg_onorm[l], mb_qnorm=mb_qnorm[l],
                 mb_knorm=mb_knorm[l], pool_w=pool_w[l].astype(_BF), pool_scale=pool_scale[l],
                 w_branch=w_branch[l].astype(_BF), w_out=w_out[l].astype(_BF), norm_ffn=norm_ffn[l],
                 w_ff1=w_ff1[l].astype(_BF), w_ff2=w_ff2[l].astype(_BF))

        attend_p = lambda q, k, v: _mb_attn(q, k, v, n_b, t_p)
        xp, kp, vp, sp_t, hp = _layer(
            xp, n_batch=n_b, t_pad=t_p, t_valid=t_p, pos0=0, chunk=HG_CHUNK, sub=HG_SUB, lb=lb_all[l],
            s0_t=s0_p, hist16=hist_p, cos_t=cos_p, sin_t=sin_p, seg=seg, attend=attend_p, w=w)

        kmean_l = jnp.transpose(kmean_all[l], (0, 2, 1, 3))

        def attend_s(q, k, v, l=l, kmean_l=kmean_l):
            idx = _gate(q, kmean_l, own_s)
            idx_flat = idx[:, :, :t_s, :MB_TOPK].reshape(-1)
            o = _sample_attn(q, k, v, cache_k, cache_v, pt_flat, idx_flat, l, n_pages, t_s, own_s)
            return jnp.transpose(o, (0, 2, 1, 3)).reshape(n_db * SAMPLE_T_PAD, MB_WIDTH)

        s0_s = jnp.swapaxes(state_hgrn[l], -1, -2)
        hist_s = jnp.pad(state_pool[l], ((0, 0), (POOL_HALO - POOL_HIST, 0), (0, 0)))
        xs, ks, vs, ss_t, hs = _layer(
            xs, n_batch=n_db, t_pad=SAMPLE_T_PAD, t_valid=t_s, pos0=past_len, chunk=SAMPLE_T_PAD,
            sub=SAMPLE_T_PAD, lb=lb_all[l], s0_t=s0_s, hist16=hist_s, cos_t=cos_s, sin_t=sin_s, seg=seg,
            attend=attend_s, w=w)

        kp_l.append(kp); vp_l.append(vp)
        ks_l.append(ks[:, :, :t_s]); vs_l.append(vs[:, :, :t_s])
        sp_l.append(jnp.swapaxes(sp_t, -1, -2)); ss_l.append(jnp.swapaxes(ss_t, -1, -2))
        hp_l.append(hp[:, POOL_HALO - POOL_HIST:]); hs_l.append(hs[:, POOL_HALO - POOL_HIST:])

    y_p = xp.reshape(n_b, t_p, D_MODEL)
    y_s = xs.reshape(n_db, SAMPLE_T_PAD, D_MODEL)[:, :t_s]
    return (y_p, y_s, jnp.stack(kp_l, axis=1), jnp.stack(vp_l, axis=1), jnp.stack(ks_l, axis=1),
            jnp.stack(vs_l, axis=1), jnp.stack(sp_l, axis=0), jnp.stack(ss_l, axis=0),
            jnp.stack(hp_l, axis=0), jnp.stack(hs_l, axis=0))
```

```python
import functools

import jax
import jax.numpy as jnp
from jax import lax
from jax.experimental import pallas as pl
from jax.experimental.pallas import tpu as pltpu

D_MODEL = 1024
PAGE_SIZE = 128
HG_HEADS = 4
HG_DK = 128
HG_DV = 128
HG_WIDTH = HG_HEADS * HG_DV
HG_CHUNK = 64
HG_SUB = 16
MB_HEADS = 8
MB_HEAD_DIM = 64
MB_WIDTH = MB_HEADS * MB_HEAD_DIM
MB_BLOCK = 256
MB_TOPK = 3
ROPE_THETA = 10000.0
POOL_WINDOWS = (2, 4, 8, 16)
POOL_GROUPS = 4
POOL_GROUP_DIM = 128
POOL_WIDTH = POOL_GROUPS * POOL_GROUP_DIM
POOL_HIST = 15
POOL_HALO = 16
N_BRANCH = 3
BRANCH_WIDTH = 512
D_FF = 4 * D_MODEL
EPS = 1e-6
NEG = -1e30
BELOW_NEG = -3e38
F_FLOOR = 1e-30
D_IN = 4 * HG_WIDTH + 3 * MB_WIDTH + POOL_WIDTH + N_BRANCH * D_MODEL
COL_QA, COL_FA, COL_IA, COL_GA, COL_QB, COL_KB, COL_VB, COL_UC = range(8)
COL_GATE_1024 = 4

LANES = 128
SUBLANES = 8
SAMPLE_T_PAD = SUBLANES
VMEM_LIMIT = 56 * 1024 * 1024
PAGES_PER_BLOCK = MB_BLOCK // PAGE_SIZE
KMEAN_PAGES_PER_STEP = 16
HEADS_PER_STEP = LANES // MB_HEAD_DIM

_BF = jnp.bfloat16
_F32 = jnp.float32
_HIGHEST = lax.Precision.HIGHEST


def _cparams(sem, vmem=VMEM_LIMIT):
    return pltpu.CompilerParams(dimension_semantics=sem, vmem_limit_bytes=vmem)


def _dot(a, b, precision=None):
    return jnp.dot(a, b, preferred_element_type=_F32, precision=precision)


def _dot_nt(a, b, precision=None):
    return lax.dot_general(a, b, (((1,), (1,)), ((), ())), preferred_element_type=_F32, precision=precision)


def _dot_tn(a, b):
    return lax.dot_general(a, b, (((0,), (0,)), ((), ())), preferred_element_type=_F32)


def _sigmoid(x):
    return 1.0 / (1.0 + jnp.exp(-x))


def _inproj_kernel(x_ref, g_ref, w_ref, z_ref, h_ref):
    @pl.when(pl.program_id(1) == 0)
    def _():
        x = x_ref[...]
        ms = jnp.mean(x * x, axis=-1, keepdims=True)
        h_ref[...] = (x * lax.rsqrt(ms + EPS) * g_ref[...]).astype(_BF)

    z_ref[...] = _dot(h_ref[...], w_ref[...])


def _inproj(x, g, w_bf, tm, tn):
    n = x.shape[0]
    return pl.pallas_call(
        _inproj_kernel,
        grid=(n // tm, D_IN // tn),
        in_specs=[
            pl.BlockSpec((tm, D_MODEL), lambda i, j: (i, 0)),
            pl.BlockSpec((1, D_MODEL), lambda i, j: (0, 0)),
            pl.BlockSpec((D_MODEL, tn), lambda i, j: (0, j)),
        ],
        out_specs=pl.BlockSpec((tm, tn), lambda i, j: (i, j)),
        out_shape=jax.ShapeDtypeStruct((n, D_IN), _F32),
        scratch_shapes=[pltpu.VMEM((tm, D_MODEL), _BF)],
        compiler_params=_cparams(("parallel", "arbitrary")),
        name="inproj",
    )(x, g.reshape(1, D_MODEL), w_bf)


def _hgrn_diag(q_s, b_s, k_s, v_s, o_s, sub):
    row8 = lax.broadcasted_iota(jnp.int32, (SUBLANES, 1), 0)
    outs = []
    for g0 in range(0, sub, SUBLANES):
        q_g = q_s[g0:g0 + SUBLANES]
        b_g = b_s[g0:g0 + SUBLANES]
        acc = o_s[g0:g0 + SUBLANES]
        for s in range(g0 + SUBLANES):
            d = b_g - b_s[s:s + 1]
            if s > g0:
                d = jnp.where(row8 >= s - g0, d, NEG)
            p = q_g * jnp.exp(d) * k_s[s:s + 1]
            acc = acc + jnp.sum(p, axis=-1, keepdims=True) * v_s[s:s + 1]
        outs.append(acc)
    return outs[0] if len(outs) == 1 else jnp.concatenate(outs, axis=0)


def _hgrn_kernel(q_ref, f_ref, i_ref, g_ref, lb_ref, on_ref, s0_ref, a_ref, s_out_ref,
                 st_scr, *, chunk, sub, t_valid):
    ci = pl.program_id(1)

    @pl.when(ci == 0)
    def _():
        st_scr[...] = s0_ref[0]

    row = lax.broadcasted_iota(jnp.int32, (chunk, 1), 0)
    tril = (lax.broadcasted_iota(jnp.int32, (chunk, chunk), 0)
            >= lax.broadcasted_iota(jnp.int32, (chunk, chunk), 1)).astype(_F32)
    n_sub = chunk // sub

    for h in range(HG_HEADS):
        sl = slice(h * HG_DK, (h + 1) * HG_DK)
        qr = q_ref[:, sl]
        fr = f_ref[:, sl]
        v = i_ref[:, sl]
        gr = g_ref[:, sl]
        lb = lb_ref[:, sl]
        q = qr * _sigmoid(qr)
        fval = lb + (1.0 - lb) * _sigmoid(fr)
        logf = jnp.log(jnp.maximum(fval, F_FLOOR))
        k = (1.0 - lb) * _sigmoid(-fr)
        if t_valid < chunk:
            valid = row < t_valid
            logf = jnp.where(valid, logf, 0.0)
            k = jnp.where(valid, k, 0.0)
        b = _dot(tril, logf, precision=_HIGHEST)
        st = st_scr[h]
        v_bf = v.astype(_BF)
        o = _dot_nt((q * jnp.exp(b)).astype(_BF), st.astype(_BF))

        o_parts = []
        for si in range(n_sub):
            r0 = si * sub
            q_s = q[r0:r0 + sub]
            b_s = b[r0:r0 + sub]
            o_s = o[r0:r0 + sub]
            if si > 0:
                ref = b[r0 - 1:r0]
                q_dec = (q_s * jnp.exp(b_s - ref)).astype(_BF)
                k_dec = (k[:r0] * jnp.exp(ref - b[:r0])).astype(_BF)
                att = _dot_nt(q_dec, k_dec)
                o_s = o_s + _dot(att.astype(_BF), v_bf[:r0])
            o_parts.append(_hgrn_diag(q_s, b_s, k[r0:r0 + sub], v[r0:r0 + sub], o_s, sub))
        o = o_parts[0] if n_sub == 1 else jnp.concatenate(o_parts, axis=0)

        b_last = b[chunk - 1:chunk]
        k_end = (k * jnp.exp(b_last - b)).astype(_BF)
        st_scr[h] = st * jnp.exp(b_last) + _dot_tn(v_bf, k_end)

        ms = jnp.mean(o * o, axis=-1, keepdims=True)
        a_ref[:, sl] = o * lax.rsqrt(ms + EPS) * on_ref[...] * (gr * _sigmoid(gr))

    @pl.when(ci == pl.num_programs(1) - 1)
    def _():
        s_out_ref[0] = st_scr[...]


def _hgrn(z, lb, onorm, s0_t, n_batch, t_pad, chunk, sub, t_valid):
    n_chunks = t_pad // chunk
    col = lambda c: pl.BlockSpec((chunk, HG_WIDTH), lambda b, ci, c=c: (b * n_chunks + ci, c))
    state_spec = pl.BlockSpec((1, HG_HEADS, HG_DV, HG_DK), lambda b, ci: (b, 0, 0, 0))
    kern = functools.partial(_hgrn_kernel, chunk=chunk, sub=sub, t_valid=t_valid)
    return pl.pallas_call(
        kern,
        grid=(n_batch, n_chunks),
        in_specs=[col(COL_QA), col(COL_FA), col(COL_IA), col(COL_GA),
                  pl.BlockSpec((1, HG_WIDTH), lambda b, ci: (0, 0)),
                  pl.BlockSpec((1, HG_DV), lambda b, ci: (0, 0)),
                  state_spec],
        out_specs=[pl.BlockSpec((chunk, HG_WIDTH), lambda b, ci: (b * n_chunks + ci, 0)), state_spec],
        out_shape=[jax.ShapeDtypeStruct((n_batch * t_pad, HG_WIDTH), _F32),
                   jax.ShapeDtypeStruct((n_batch, HG_HEADS, HG_DV, HG_DK), _F32)],
        scratch_shapes=[pltpu.VMEM((HG_HEADS, HG_DV, HG_DK), _F32)],
        compiler_params=_cparams(("parallel", "arbitrary")),
        name="hgrn",
    )(z, z, z, z, lb.reshape(1, HG_WIDTH), onorm.reshape(1, HG_DV), s0_t)


def _norm_rope(x, w, cos, sin, seg):
    lane = lax.broadcasted_iota(jnp.int32, (1, LANES), 1)
    first_half = (lane % MB_HEAD_DIM) < (MB_HEAD_DIM // 2)
    sq = x * x
    hi = sq.astype(_BF)
    lo = (sq - hi.astype(_F32)).astype(_BF)
    ss = _dot(hi, seg) + _dot(lo, seg)
    xn = x * lax.rsqrt(ss * (1.0 / MB_HEAD_DIM) + EPS) * w
    outs = []
    for g in range(MB_WIDTH // LANES):
        xg = xn[:, g * LANES:(g + 1) * LANES]
        rot = jnp.where(first_half, pltpu.roll(xg, LANES - MB_HEAD_DIM // 2, 1),
                        pltpu.roll(xg, MB_HEAD_DIM // 2, 1))
        outs.append(xg * cos + rot * sin)
    return outs


def _mb_prep_kernel(q_ref, k_ref, v_ref, cos_ref, sin_ref, qn_ref, kn_ref, seg_ref,
                    qo_ref, ko_ref, vo_ref):
    q_g = _norm_rope(q_ref[...], qn_ref[...], cos_ref[...], sin_ref[...], seg_ref[...])
    k_g = _norm_rope(k_ref[...], kn_ref[...], cos_ref[...], sin_ref[...], seg_ref[...])
    v = v_ref[...]
    for h in range(MB_HEADS):
        g, off = divmod(h * MB_HEAD_DIM, LANES)
        qo_ref[0, h] = q_g[g][:, off:off + MB_HEAD_DIM]
        ko_ref[0, h] = k_g[g][:, off:off + MB_HEAD_DIM]
        vo_ref[0, h] = v[:, h * MB_HEAD_DIM:(h + 1) * MB_HEAD_DIM]


def _mb_prep_t_kernel(q_ref, k_ref, v_ref, cos_ref, sin_ref, qn_ref, kn_ref, seg_ref, *rest):
    qo_ref, ko_ref, vo_ref = rest[-3:]
    q_g = _norm_rope(q_ref[...], qn_ref[...], cos_ref[...], sin_ref[...], seg_ref[...])
    k_g = _norm_rope(k_ref[...], kn_ref[...], cos_ref[...], sin_ref[...], seg_ref[...])
    v = v_ref[...]
    for g in range(MB_WIDTH // LANES):
        q_t = q_g[g].T
        k_t = k_g[g].T
        v_t = v[:, g * LANES:(g + 1) * LANES].T
        for hh in range(HEADS_PER_STEP):
            h = g * HEADS_PER_STEP + hh
            rows = slice(hh * MB_HEAD_DIM, (hh + 1) * MB_HEAD_DIM)
            qo_ref[0, h] = q_t[rows]
            ko_ref[0, 0, h] = k_t[rows]
            vo_ref[0, 0, h] = v_t[rows]


def _prep_common_specs(n_t, tm):
    col = lambda c: pl.BlockSpec((tm, MB_WIDTH), lambda b, ti, c=c: (b * n_t + ti, c))
    tab = pl.BlockSpec((tm, LANES), lambda b, ti: (ti, 0))
    vec = pl.BlockSpec((1, MB_WIDTH), lambda b, ti: (0, 0))
    return [col(COL_QB), col(COL_KB), col(COL_VB), tab, tab, vec, vec,
            pl.BlockSpec((MB_WIDTH, MB_WIDTH), lambda b, ti: (0, 0))]


def _mb_prep(z, cos_t, sin_t, qn, kn, seg, n_batch, t_pad, tm):
    n_t = t_pad // tm
    out = pl.BlockSpec((1, MB_HEADS, tm, MB_HEAD_DIM), lambda b, ti: (b, 0, ti, 0))
    shp = jax.ShapeDtypeStruct((n_batch, MB_HEADS, t_pad, MB_HEAD_DIM), _F32)
    return pl.pallas_call(
        _mb_prep_kernel,
        grid=(n_batch, n_t),
        in_specs=_prep_common_specs(n_t, tm),
        out_specs=[out, out, out],
        out_shape=[shp, shp, shp],
        compiler_params=_cparams(("parallel", "parallel")),
        name="mb_prep",
    )(z, z, z, cos_t, sin_t, jnp.tile(qn, MB_HEADS).reshape(1, MB_WIDTH),
      jnp.tile(kn, MB_HEADS).reshape(1, MB_WIDTH), seg)


def _mb_prep_t(z, cos_t, sin_t, qn, kn, seg, n_batch, t, tm, layer, depth, kv_prev):
    n_t = t // tm
    q_spec = pl.BlockSpec((1, MB_HEADS, MB_HEAD_DIM, tm), lambda b, ti: (b, 0, 0, ti))
    kv_spec = pl.BlockSpec((1, 1, MB_HEADS, MB_HEAD_DIM, tm), lambda b, ti: (b, layer, 0, 0, ti))
    kv_shape = jax.ShapeDtypeStruct((n_batch, depth, MB_HEADS, MB_HEAD_DIM, t), _F32)
    in_specs = _prep_common_specs(n_t, tm)
    n_in = len(in_specs)
    extra, aliases = (), {}
    if kv_prev is not None:
        extra = tuple(kv_prev)
        in_specs = in_specs + [pl.BlockSpec(memory_space=pl.ANY)] * 2
        aliases = {n_in: 1, n_in + 1: 2}
    return pl.pallas_call(
        _mb_prep_t_kernel,
        grid=(n_batch, n_t),
        in_specs=in_specs,
        out_specs=[q_spec, kv_spec, kv_spec],
        out_shape=[jax.ShapeDtypeStruct((n_batch, MB_HEADS, MB_HEAD_DIM, t), _F32), kv_shape, kv_shape],
        input_output_aliases=aliases,
        compiler_params=_cparams(("parallel", "parallel")),
        name="mb_prep_t",
    )(z, z, z, cos_t, sin_t, jnp.tile(qn, MB_HEADS).reshape(1, MB_WIDTH),
      jnp.tile(kn, MB_HEADS).reshape(1, MB_WIDTH), seg, *extra)


def _top3_select(gate, axis):
    n = gate.shape[axis]
    idx = lax.broadcasted_iota(jnp.int32, gate.shape, axis).astype(_F32)
    sel = jnp.zeros(gate.shape, _F32)
    picks = []
    g = gate
    for _ in range(MB_TOPK):
        m = jnp.max(g, axis=axis, keepdims=True)
        first = jnp.min(jnp.where(g == m, idx, float(n)), axis=axis, keepdims=True)
        pick = idx == first
        sel = jnp.where(pick, 1.0, sel)
        picks.append(first)
        g = jnp.where(pick, BELOW_NEG, g)
    return sel, picks


AUG_DEPTH = 2 * MB_HEAD_DIM
V_AUG = LANES
LOG2_E = 1.4426950408889634


def _mb_attn_kernel(qt_ref, kt_ref, vt_ref, o_ref, kt_aug_scr, v_aug_scr, km_scr, s_scr, *, n_blk):
    i = pl.program_id(2)
    heads = range(HEADS_PER_STEP)
    filler = n_blk
    ext_rows = AUG_DEPTH - MB_HEAD_DIM

    @pl.when(i == 0)
    def _():
        lane = lax.broadcasted_iota(jnp.int32, (MB_HEAD_DIM, LANES), 1)
        ext_row = lax.broadcasted_iota(jnp.int32, (ext_rows, MB_BLOCK), 0)
        ones_row = jnp.where(lax.broadcasted_iota(jnp.int32, (V_AUG - MB_HEAD_DIM, MB_BLOCK), 0) == 0, 1.0, 0.0)
        for h in heads:
            km_t = jnp.zeros((MB_HEAD_DIM, LANES), _F32)
            for j in range(n_blk):
                kt_blk = kt_ref[0, 0, h, :, j * MB_BLOCK:(j + 1) * MB_BLOCK]
                vt_blk = vt_ref[0, 0, h, :, j * MB_BLOCK:(j + 1) * MB_BLOCK]
                kt_aug_scr[h, j] = jnp.concatenate([kt_blk, jnp.where(ext_row == j, 1.0, 0.0)], axis=0).astype(_BF)
                v_aug_scr[h, j] = jnp.concatenate([vt_blk, ones_row], axis=0).T.astype(_BF)
                mean = jnp.sum(kt_blk, axis=1, keepdims=True) * (1.0 / MB_BLOCK)
                km_t = jnp.where(lane == j, mean, km_t)
            km_scr[h] = km_t.T[:n_blk]
            kt_aug_scr[h, filler] = jnp.concatenate(
                [jnp.zeros((MB_HEAD_DIM, MB_BLOCK), _F32), jnp.where(ext_row == filler, 1.0, 0.0)],
                axis=0).astype(_BF)
            v_aug_scr[h, filler] = jnp.zeros((MB_BLOCK, V_AUG), _BF)

    blk_row = lax.broadcasted_iota(jnp.int32, (n_blk, MB_BLOCK), 0)
    row8 = lax.broadcasted_iota(jnp.int32, (SUBLANES, MB_BLOCK), 0)
    q_aug = []
    for h in heads:
        q_t = qt_ref[0, h]
        gate = _dot(km_scr[h], q_t, precision=_HIGHEST)
        eligible = blk_row < i
        sel, _ = _top3_select(jnp.where(eligible, gate, NEG), 0)
        sel = jnp.where(jnp.logical_or(blk_row == i, eligible), jnp.where(blk_row == i, 1.0, sel), 0.0)
        bias = (sel - 1.0) * (-NEG)
        q_s = q_t * (MB_HEAD_DIM ** -0.5 * LOG2_E)
        tail = jnp.where(row8 == 0, NEG, 0.0)
        pad = jnp.zeros((ext_rows - n_blk - SUBLANES, MB_BLOCK), _F32)
        q_aug_t = jnp.concatenate([q_s, bias, tail, pad], axis=0)
        q_aug.append(q_aug_t.T.astype(_BF))

    n_pairs = (i + 1) // 2

    def pair_blocks(jj):
        j0 = 2 * jj
        return (j0, jnp.where(j0 + 1 < i, j0 + 1, filler))

    def fold_max(s):
        return jnp.maximum(s[:, :LANES], s[:, LANES:])

    query_ge_key = (lax.broadcasted_iota(jnp.int32, (MB_BLOCK, MB_BLOCK), 0)
                    >= lax.broadcasted_iota(jnp.int32, (MB_BLOCK, MB_BLOCK), 1))
    m_init = []
    for h in heads:
        s = jnp.where(query_ge_key, _dot(q_aug[h], kt_aug_scr[h, i]), NEG)
        s_scr[h, i] = s
        m_init.append(fold_max(s))

    def score_body(jj, m):
        m = list(m)
        for j in pair_blocks(jj):
            for h in heads:
                s = _dot(q_aug[h], kt_aug_scr[h, j])
                s_scr[h, j] = s
                m[h] = jnp.maximum(m[h], fold_max(s))
        return tuple(m)

    m = lax.fori_loop(0, n_pairs, score_body, tuple(m_init))
    m_b = [jnp.broadcast_to(jnp.max(m[h], axis=1, keepdims=True), (MB_BLOCK, LANES)) for h in heads]

    def weighted_values(h, j):
        s = s_scr[h, j]
        p = jnp.concatenate([jnp.exp2(s[:, :LANES] - m_b[h]), jnp.exp2(s[:, LANES:] - m_b[h])], axis=1)
        return _dot(p.astype(_BF), v_aug_scr[h, j])

    def acc_body(jj, acc):
        acc = list(acc)
        for j in pair_blocks(jj):
            for h in heads:
                acc[h] = acc[h] + weighted_values(h, j)
        return tuple(acc)

    acc = lax.fori_loop(0, n_pairs, acc_body, tuple(weighted_values(h, i) for h in heads))
    o_ref[...] = jnp.concatenate(
        [acc[h][:, :MB_HEAD_DIM] / acc[h][:, MB_HEAD_DIM:MB_HEAD_DIM + 1] for h in heads], axis=1)


def _mb_attn(q_t, k_t_all, v_t_all, layer, n_batch, t):
    n_blk = t // MB_BLOCK
    assert n_blk % SUBLANES == 0 and MB_HEAD_DIM + n_blk + SUBLANES <= AUG_DEPTH
    n_hp = MB_HEADS // HEADS_PER_STEP
    kern = functools.partial(_mb_attn_kernel, n_blk=n_blk)
    kv_spec = pl.BlockSpec((1, 1, HEADS_PER_STEP, MB_HEAD_DIM, t), lambda b, hp, i: (b, layer, hp, 0, 0))
    return pl.pallas_call(
        kern,
        grid=(n_batch, n_hp, n_blk),
        in_specs=[pl.BlockSpec((1, HEADS_PER_STEP, MB_HEAD_DIM, MB_BLOCK), lambda b, hp, i: (b, hp, 0, i)),
                  kv_spec, kv_spec],
        out_specs=pl.BlockSpec((MB_BLOCK, LANES), lambda b, hp, i: (b * n_blk + i, hp)),
        out_shape=jax.ShapeDtypeStruct((n_batch * t, MB_WIDTH), _F32),
        scratch_shapes=[pltpu.VMEM((HEADS_PER_STEP, n_blk + 1, AUG_DEPTH, MB_BLOCK), _BF),
                        pltpu.VMEM((HEADS_PER_STEP, n_blk + 1, MB_BLOCK, V_AUG), _BF),
                        pltpu.VMEM((HEADS_PER_STEP, n_blk, MB_HEAD_DIM), _F32),
                        pltpu.VMEM((HEADS_PER_STEP, n_blk + 1, MB_BLOCK, MB_BLOCK), _F32)],
        compiler_params=_cparams(("parallel", "parallel", "arbitrary")),
        name="mb_attn",
    )(q_t, k_t_all, v_t_all)


def _kmean_kernel(pt_ref, *refs):
    page_refs, out_ref = refs[:KMEAN_PAGES_PER_STEP], refs[KMEAN_PAGES_PER_STEP]
    for jb in range(KMEAN_PAGES_PER_STEP // PAGES_PER_BLOCK):
        tot = None
        for p in range(PAGES_PER_BLOCK):
            page = page_refs[jb * PAGES_PER_BLOCK + p][0, 0]
            tot = page if tot is None else tot + page
        out_ref[0, 0, 0, :, :, jb:jb + 1] = jnp.sum(tot, axis=-1, keepdims=True) * (1.0 / MB_BLOCK)


def _kmean(cache_kt, pt_flat, n_layers, n_batch, n_pages):
    n_steps = n_pages // KMEAN_PAGES_PER_STEP
    blocks_per_step = KMEAN_PAGES_PER_STEP // PAGES_PER_BLOCK

    def page_spec(p):
        return pl.BlockSpec((1, 1, MB_HEADS, MB_HEAD_DIM, PAGE_SIZE),
                            lambda l, b, c, pt, p=p: (pt[b * n_pages + c * KMEAN_PAGES_PER_STEP + p], l, 0, 0, 0))

    grid_spec = pltpu.PrefetchScalarGridSpec(
        num_scalar_prefetch=1,
        grid=(n_layers, n_batch, n_steps),
        in_specs=[page_spec(p) for p in range(KMEAN_PAGES_PER_STEP)],
        out_specs=pl.BlockSpec((1, 1, 1, MB_HEADS, MB_HEAD_DIM, blocks_per_step),
                               lambda l, b, c, pt: (l, b, c, 0, 0, 0)),
    )
    return pl.pallas_call(
        _kmean_kernel,
        grid_spec=grid_spec,
        out_shape=jax.ShapeDtypeStruct((n_layers, n_batch, n_steps, MB_HEADS, MB_HEAD_DIM, blocks_per_step), _F32),
        compiler_params=_cparams(("parallel", "parallel", "parallel")),
        name="kmean",
    )(pt_flat, *([cache_kt] * KMEAN_PAGES_PER_STEP))


def _gate_kernel(q_ref, km_ref, idx_ref, *, own):
    n_past = km_ref.shape[3]
    col = lax.broadcasted_iota(jnp.int32, (SAMPLE_T_PAD, n_past), 1)
    lane = lax.broadcasted_iota(jnp.int32, (SAMPLE_T_PAD, LANES), 1)
    for h in range(MB_HEADS):
        gate = _dot(q_ref[0, h], km_ref[0, h], precision=_HIGHEST)
        _, picks = _top3_select(jnp.where(col < own, gate, NEG), 1)
        out = jnp.zeros((SAMPLE_T_PAD, LANES), jnp.int32)
        for r, first in enumerate(picks):
            out = jnp.where(lane == r, first.astype(jnp.int32), out)
        idx_ref[0, h] = out


def _gate(q_s, kmean_l, own):
    n_batch = q_s.shape[0]
    n_past = kmean_l.shape[3]
    return pl.pallas_call(
        functools.partial(_gate_kernel, own=own),
        grid=(n_batch,),
        in_specs=[pl.BlockSpec((1, MB_HEADS, SAMPLE_T_PAD, MB_HEAD_DIM), lambda b: (b, 0, 0, 0)),
                  pl.BlockSpec((1, MB_HEADS, MB_HEAD_DIM, n_past), lambda b: (b, 0, 0, 0))],
        out_specs=pl.BlockSpec((1, MB_HEADS, SAMPLE_T_PAD, LANES), lambda b: (b, 0, 0, 0)),
        out_shape=jax.ShapeDtypeStruct((n_batch, MB_HEADS, SAMPLE_T_PAD, LANES), jnp.int32),
        compiler_params=_cparams(("parallel",)),
        name="mb_gate",
    )(q_s, kmean_l)


def _sample_attn_kernel(pt_ref, idx_ref, q_ref, kn_ref, vn_ref, *refs, t_s, own):
    n_slots = t_s * MB_TOPK * PAGES_PER_BLOCK
    k_pages, v_pages, o_ref = refs[:n_slots], refs[n_slots:2 * n_slots], refs[2 * n_slots]
    b = pl.program_id(0)
    h = pl.program_id(1)
    q = q_ref[0, 0] * (MB_HEAD_DIM ** -0.5)
    q_bf = q.astype(_BF)
    row = lax.broadcasted_iota(jnp.int32, (SAMPLE_T_PAD, 1), 0)

    scores = []
    for t in range(t_s):
        for j in range(MB_TOPK):
            blk = idx_ref[((b * MB_HEADS + h) * t_s + t) * MB_TOPK + j]
            allowed = row == jnp.where(blk < own, t, -1)
            for p in range(PAGES_PER_BLOCK):
                slot = (t * MB_TOPK + j) * PAGES_PER_BLOCK + p
                s = _dot(q_bf, k_pages[slot][0, 0, 0].astype(_BF))
                scores.append(jnp.where(allowed, s, NEG))
    kn = kn_ref[0, 0]
    vn = vn_ref[0, 0]
    own_scores = []
    for c in range(t_s):
        s = jnp.sum(q * kn[c:c + 1, :], axis=1, keepdims=True)
        own_scores.append(jnp.where(row >= c, s, NEG))

    m = own_scores[0]
    for s in own_scores[1:]:
        m = jnp.maximum(m, s)
    for s in scores:
        m = jnp.maximum(m, jnp.max(s, axis=1, keepdims=True))
    l = jnp.zeros((SAMPLE_T_PAD, 1), _F32)
    acc = jnp.zeros((SAMPLE_T_PAD, MB_HEAD_DIM), _F32)
    for c, s in enumerate(own_scores):
        p = jnp.exp(s - m)
        l = l + p
        acc = acc + p * vn[c:c + 1, :]
    for slot, s in enumerate(scores):
        p = jnp.exp(s - m)
        l = l + jnp.sum(p, axis=1, keepdims=True)
        acc = acc + _dot_nt(p.astype(_BF), v_pages[slot][0, 0, 0].astype(_BF))
    o_ref[0, 0] = acc / l


def _sample_attn(q_s, k_s, v_s, cache_kt, cache_vt, pt_flat, idx_flat, layer, n_pages, t_s, own):
    n_batch = q_s.shape[0]
    new_spec = pl.BlockSpec((1, 1, SAMPLE_T_PAD, MB_HEAD_DIM), lambda b, h, pt, idx: (b, h, 0, 0))

    def page_spec(t, j, p):
        def index_map(b, h, pt, idx):
            blk = idx[((b * MB_HEADS + h) * t_s + t) * MB_TOPK + j]
            return (pt[b * n_pages + blk * PAGES_PER_BLOCK + p], layer, h, 0, 0)
        return pl.BlockSpec((1, 1, 1, MB_HEAD_DIM, PAGE_SIZE), index_map)

    page_specs = [page_spec(t, j, p) for t in range(t_s) for j in range(MB_TOPK) for p in range(PAGES_PER_BLOCK)]
    n_slots = len(page_specs)
    grid_spec = pltpu.PrefetchScalarGridSpec(
        num_scalar_prefetch=2,
        grid=(n_batch, MB_HEADS),
        in_specs=[new_spec, new_spec, new_spec] + page_specs + page_specs,
        out_specs=new_spec,
    )
    return pl.pallas_call(
        functools.partial(_sample_attn_kernel, t_s=t_s, own=own),
        grid_spec=grid_spec,
        out_shape=jax.ShapeDtypeStruct((n_batch, MB_HEADS, SAMPLE_T_PAD, MB_HEAD_DIM), _F32),
        compiler_params=_cparams(("parallel", "parallel")),
        name="mb_sample_attn",
    )(pt_flat, idx_flat, q_s, k_s, v_s, *([cache_kt] * n_slots), *([cache_vt] * n_slots))


def _pool_kernel(u_ref, hist_ref, w_ref, sc_ref, y_ref, hist_out_ref, ext_scr, *, tm, pos0, t_last):
    ti = pl.program_id(1)

    @pl.when(ti == 0)
    def _():
        ext_scr[0:POOL_HALO, :] = hist_ref[0]

    u = u_ref[...]
    ext_scr[POOL_HALO:POOL_HALO + tm, :] = u
    pos = pos0 + ti * tm + lax.broadcasted_iota(jnp.int32, (tm, 1), 0)
    for g, w in enumerate(POOL_WINDOWS):
        sl = slice(g * POOL_GROUP_DIM, (g + 1) * POOL_GROUP_DIM)
        tot = u[:, sl]
        for d in range(1, w):
            tot = tot + ext_scr[POOL_HALO - d:POOL_HALO - d + tm, sl]
        cnt = jnp.minimum(pos + 1, w).astype(_F32)
        diff = tot / cnt - u[:, sl]
        y = _dot(diff.astype(_BF), w_ref[g])
        y_ref[:, sl] = y * sc_ref[:, sl]

    @pl.when(ti == pl.num_programs(1) - 1)
    def _():
        hist_out_ref[0] = ext_scr[t_last:t_last + POOL_HALO, :]

    @pl.when(ti < pl.num_programs(1) - 1)
    def _():
        ext_scr[0:POOL_HALO, :] = ext_scr[tm:tm + POOL_HALO, :]


def _pool(z, hist16, pool_w_bf, pool_scale, n_batch, t_pad, tm, pos0, t_valid):
    n_t = t_pad // tm
    t_last = t_valid - (n_t - 1) * tm
    hist_spec = pl.BlockSpec((1, POOL_HALO, POOL_WIDTH), lambda b, ti: (b, 0, 0))
    kern = functools.partial(_pool_kernel, tm=tm, pos0=pos0, t_last=t_last)
    return pl.pallas_call(
        kern,
        grid=(n_batch, n_t),
        in_specs=[pl.BlockSpec((tm, POOL_WIDTH), lambda b, ti: (b * n_t + ti, COL_UC)),
                  hist_spec,
                  pl.BlockSpec((POOL_GROUPS, POOL_GROUP_DIM, POOL_GROUP_DIM), lambda b, ti: (0, 0, 0)),
                  pl.BlockSpec((1, POOL_WIDTH), lambda b, ti: (0, 0))],
        out_specs=[pl.BlockSpec((tm, POOL_WIDTH), lambda b, ti: (b * n_t + ti, 0)), hist_spec],
        out_shape=[jax.ShapeDtypeStruct((n_batch * t_pad, POOL_WIDTH), _F32),
                   jax.ShapeDtypeStruct((n_batch, POOL_HALO, POOL_WIDTH), _F32)],
        scratch_shapes=[pltpu.VMEM((POOL_HALO + tm, POOL_WIDTH), _F32)],
        compiler_params=_cparams(("parallel", "arbitrary")),
        name="pool",
    )(z, hist16, pool_w_bf, pool_scale.reshape(1, POOL_WIDTH))


def _merge_kernel(x_ref, a_ref, b_ref, c_ref, g0_ref, g1_ref, g2_ref, wb_ref, wo_ref, o_ref):
    merged = None
    for br_ref, g_ref, n in ((a_ref, g0_ref, 0), (b_ref, g1_ref, 1), (c_ref, g2_ref, 2)):
        proj = _dot(br_ref[...].astype(_BF), wb_ref[n])
        term = _sigmoid(g_ref[...]) * proj
        merged = term if merged is None else merged + term
    o_ref[...] = x_ref[...] + _dot(merged.astype(_BF), wo_ref[...])


def _merge(x, a, b, c, z, wb_bf, wo_bf, tm):
    n = x.shape[0]
    row = lambda w: pl.BlockSpec((tm, w), lambda i: (i, 0))
    gate = lambda k: pl.BlockSpec((tm, D_MODEL), lambda i, k=k: (i, COL_GATE_1024 + k))
    return pl.pallas_call(
        _merge_kernel,
        grid=(n // tm,),
        in_specs=[row(D_MODEL), row(BRANCH_WIDTH), row(BRANCH_WIDTH), row(BRANCH_WIDTH),
                  gate(0), gate(1), gate(2),
                  pl.BlockSpec((N_BRANCH, BRANCH_WIDTH, D_MODEL), lambda i: (0, 0, 0)),
                  pl.BlockSpec((D_MODEL, D_MODEL), lambda i: (0, 0))],
        out_specs=row(D_MODEL),
        out_shape=jax.ShapeDtypeStruct((n, D_MODEL), _F32),
        compiler_params=_cparams(("parallel",)),
        name="merge",
    )(x, a, b, c, z, z, z, wb_bf, wo_bf)


def _ffn_kernel(x_ref, g_ref, w1_ref, w2_ref, o_ref):
    x = x_ref[...]
    ms = jnp.mean(x * x, axis=-1, keepdims=True)
    h = (x * lax.rsqrt(ms + EPS) * g_ref[...]).astype(_BF)
    hid = jnp.maximum(_dot(h, w1_ref[...]), 0.0)
    hid = (hid * hid).astype(_BF)
    o_ref[...] = x + _dot(hid, w2_ref[...])


def _ffn(x, g, w1_bf, w2_bf, tm):
    n = x.shape[0]
    row = pl.BlockSpec((tm, D_MODEL), lambda i: (i, 0))
    return pl.pallas_call(
        _ffn_kernel,
        grid=(n // tm,),
        in_specs=[row, pl.BlockSpec((1, D_MODEL), lambda i: (0, 0)),
                  pl.BlockSpec((D_MODEL, D_FF), lambda i: (0, 0)),
                  pl.BlockSpec((D_FF, D_MODEL), lambda i: (0, 0))],
        out_specs=row,
        out_shape=jax.ShapeDtypeStruct((n, D_MODEL), _F32),
        compiler_params=_cparams(("parallel",)),
        name="ffn",
    )(x, g.reshape(1, D_MODEL), w1_bf, w2_bf)


def _rope_tables(pos):
    half = MB_HEAD_DIM // 2
    inv = ROPE_THETA ** (-jnp.arange(half, dtype=_F32) / half)
    ang = pos.astype(_F32)[:, None] * inv[None, :]
    cos, sin = jnp.cos(ang), jnp.sin(ang)
    cos_t = jnp.concatenate([cos, cos, cos, cos], axis=1)
    sin_t = jnp.concatenate([-sin, sin, -sin, sin], axis=1)
    return cos_t, sin_t


def _tile_rows(n, cap):
    t = cap
    while n % t:
        t //= 2
    return t


def _layer(x, *, n_batch, t_pad, t_valid, pos0, chunk, sub, lb, s0_t, hist16, moba, w):
    n = x.shape[0]
    z = _inproj(x, w["norm_mix"], w["w_in"], _tile_rows(n, 1024), D_IN // 4)
    a_out, s_new_t = _hgrn(z, lb, w["hg_onorm"], s0_t, n_batch, t_pad, chunk, sub, t_valid)
    b_out, kv = moba(z)
    c_out, hist_new = _pool(z, hist16, w["pool_w"], w["pool_scale"], n_batch, t_pad, _tile_rows(t_pad, 512),
                            pos0, t_valid)
    x1 = _merge(x, a_out, b_out, c_out, z, w["w_branch"], w["w_out"], _tile_rows(n, 512))
    x2 = _ffn(x1, w["norm_ffn"], w["w_ff1"], w["w_ff2"], _tile_rows(n, 256))
    return x2, kv, s_new_t, hist_new


def kernel(x_prompt, x_sample, cache_k, cache_v, state_hgrn, state_pool, page_table, norm_mix, w_in, hg_lb,
           hg_onorm, mb_qnorm, mb_knorm, pool_w, pool_scale, w_branch, w_out, norm_ffn, w_ff1, w_ff2):
    n_b, t_p, _ = x_prompt.shape
    n_db, t_s, _ = x_sample.shape
    depth = w_in.shape[0]
    n_pages = page_table.shape[1]
    past_len = n_pages * PAGE_SIZE
    assert t_p % MB_BLOCK == 0 and t_p % HG_CHUNK == 0
    assert past_len % MB_BLOCK == 0 and t_s <= SAMPLE_T_PAD and n_pages % KMEAN_PAGES_PER_STEP == 0
    own_s = past_len // MB_BLOCK

    lb_soft = jax.nn.softmax(hg_lb.astype(_F32), axis=0)
    lb_all = jnp.cumsum(lb_soft, axis=0) - lb_soft[0:1]

    seg = (jnp.arange(MB_WIDTH)[:, None] // MB_HEAD_DIM == jnp.arange(MB_WIDTH)[None, :] // MB_HEAD_DIM).astype(_BF)
    cos_p, sin_p = _rope_tables(jnp.arange(t_p, dtype=jnp.int32))
    cos_s, sin_s = _rope_tables(past_len + jnp.arange(SAMPLE_T_PAD, dtype=jnp.int32))

    cache_kt = jnp.swapaxes(cache_k, 3, 4)
    cache_vt = jnp.swapaxes(cache_v, 3, 4)
    pt_flat = page_table.reshape(-1).astype(jnp.int32)
    kmean_all = _kmean(cache_kt, pt_flat, depth, n_db, n_pages)
    kmean_all = jnp.transpose(kmean_all, (0, 1, 3, 4, 2, 5)).reshape(depth, n_db, MB_HEADS, MB_HEAD_DIM, own_s)

    xp = x_prompt.reshape(n_b * t_p, D_MODEL)
    xs = jnp.pad(x_sample, ((0, 0), (0, SAMPLE_T_PAD - t_s), (0, 0))).reshape(n_db * SAMPLE_T_PAD, D_MODEL)

    s0_p = jnp.zeros((n_b, HG_HEADS, HG_DV, HG_DK), _F32)
    hist_p = jnp.zeros((n_b, POOL_HALO, POOL_WIDTH), _F32)

    kv_p = None
    ks_l, vs_l, sp_l, ss_l, hp_l, hs_l = [], [], [], [], [], []
    for l in range(depth):
        w = dict(norm_mix=norm_mix[l], w_in=w_in[l].astype(_BF), hg_onorm=hg_onorm[l], mb_qnorm=mb_qnorm[l],
                 mb_knorm=mb_knorm[l], pool_w=pool_w[l].astype(_BF), pool_scale=pool_scale[l],
                 w_branch=w_branch[l].astype(_BF), w_out=w_out[l].astype(_BF), norm_ffn=norm_ffn[l],
                 w_ff1=w_ff1[l].astype(_BF), w_ff2=w_ff2[l].astype(_BF))

        def moba_p(z, l=l, w=w, kv_prev=kv_p):
            q_t, k_t_all, v_t_all = _mb_prep_t(z, cos_p, sin_p, w["mb_qnorm"], w["mb_knorm"], seg, n_b, t_p,
                                               _tile_rows(t_p, 512), l, depth, kv_prev)
            return _mb_attn(q_t, k_t_all, v_t_all, l, n_b, t_p), (k_t_all, v_t_all)

        xp, kv_p, sp_t, hp = _layer(
            xp, n_batch=n_b, t_pad=t_p, t_valid=t_p, pos0=0, chunk=HG_CHUNK, sub=HG_SUB, lb=lb_all[l],
            s0_t=s0_p, hist16=hist_p, moba=moba_p, w=w)

        def moba_s(z, l=l, w=w):
            q, k, v = _mb_prep(z, cos_s, sin_s, w["mb_qnorm"], w["mb_knorm"], seg, n_db, SAMPLE_T_PAD, SAMPLE_T_PAD)
            idx = _gate(q, kmean_all[l], own_s)
            idx_flat = idx[:, :, :t_s, :MB_TOPK].reshape(-1)
            o = _sample_attn(q, k, v, cache_kt, cache_vt, pt_flat, idx_flat, l, n_pages, t_s, own_s)
            return jnp.transpose(o, (0, 2, 1, 3)).reshape(n_db * SAMPLE_T_PAD, MB_WIDTH), (k, v)

        s0_s = jnp.swapaxes(state_hgrn[l], -1, -2)
        hist_s = jnp.pad(state_pool[l], ((0, 0), (POOL_HALO - POOL_HIST, 0), (0, 0)))
        xs, (ks, vs), ss_t, hs = _layer(
            xs, n_batch=n_db, t_pad=SAMPLE_T_PAD, t_valid=t_s, pos0=past_len, chunk=SAMPLE_T_PAD,
            sub=SAMPLE_T_PAD, lb=lb_all[l], s0_t=s0_s, hist16=hist_s, moba=moba_s, w=w)

        ks_l.append(ks[:, :, :t_s]); vs_l.append(vs[:, :, :t_s])
        sp_l.append(jnp.swapaxes(sp_t, -1, -2)); ss_l.append(jnp.swapaxes(ss_t, -1, -2))
        hp_l.append(hp[:, POOL_HALO - POOL_HIST:]); hs_l.append(hs[:, POOL_HALO - POOL_HIST:])

    y_p = xp.reshape(n_b, t_p, D_MODEL)
    y_s = xs.reshape(n_db, SAMPLE_T_PAD, D_MODEL)[:, :t_s]
    k_p = jnp.swapaxes(kv_p[0], 3, 4)
    v_p = jnp.swapaxes(kv_p[1], 3, 4)
    return (y_p, y_s, k_p, v_p, jnp.stack(ks_l, axis=1), jnp.stack(vs_l, axis=1),
            jnp.stack(sp_l, axis=0), jnp.stack(ss_l, axis=0), jnp.stack(hp_l, axis=0), jnp.stack(hs_l, axis=0))
```

```python
import functools

import jax
import jax.numpy as jnp
from jax import lax
from jax.experimental import pallas as pl
from jax.experimental.pallas import tpu as pltpu

D_MODEL = 1024
PAGE_SIZE = 128
HG_HEADS = 4
HG_DK = 128
HG_DV = 128
HG_WIDTH = HG_HEADS * HG_DV
HG_CHUNK = 64
HG_SUB = 16
HG_CHUNKS_PER_STEP = 1
MB_HEADS = 8
MB_HEAD_DIM = 64
MB_WIDTH = MB_HEADS * MB_HEAD_DIM
MB_BLOCK = 256
MB_TOPK = 3
ROPE_THETA = 10000.0
POOL_WINDOWS = (2, 4, 8, 16)
POOL_GROUPS = 4
POOL_GROUP_DIM = 128
POOL_WIDTH = POOL_GROUPS * POOL_GROUP_DIM
POOL_HIST = 15
POOL_HALO = 16
N_BRANCH = 3
BRANCH_WIDTH = 512
D_FF = 4 * D_MODEL
EPS = 1e-6
NEG = -1e30
BELOW_NEG = -3e38
F_FLOOR = 1e-30
D_IN = 4 * HG_WIDTH + 3 * MB_WIDTH + POOL_WIDTH + N_BRANCH * D_MODEL
COL_QA, COL_FA, COL_IA, COL_GA, COL_QB, COL_KB, COL_VB, COL_UC = range(8)
COL_GATE_1024 = 4

LANES = 128
SUBLANES = 8
SAMPLE_T_PAD = SUBLANES
VMEM_LIMIT = 56 * 1024 * 1024
PAGES_PER_BLOCK = MB_BLOCK // PAGE_SIZE
KMEAN_PAGES_PER_STEP = 32
HEADS_PER_STEP = LANES // MB_HEAD_DIM

_BF = jnp.bfloat16
_F32 = jnp.float32
_HIGHEST = lax.Precision.HIGHEST


def _cparams(sem, vmem=VMEM_LIMIT):
    return pltpu.CompilerParams(dimension_semantics=sem, vmem_limit_bytes=vmem)


def _dot(a, b, precision=None):
    return jnp.dot(a, b, preferred_element_type=_F32, precision=precision)


def _dot_nt(a, b, precision=None):
    return lax.dot_general(a, b, (((1,), (1,)), ((), ())), preferred_element_type=_F32, precision=precision)


def _dot_tn(a, b):
    return lax.dot_general(a, b, (((0,), (0,)), ((), ())), preferred_element_type=_F32)


def _sigmoid(x):
    return 1.0 / (1.0 + jnp.exp(-x))


def _inproj_kernel(x_ref, g_ref, w_ref, z_ref, h_ref):
    @pl.when(pl.program_id(1) == 0)
    def _():
        x = x_ref[...]
        ms = jnp.mean(x * x, axis=-1, keepdims=True)
        h_ref[...] = (x * lax.rsqrt(ms + EPS) * g_ref[...]).astype(_BF)

    z_ref[...] = _dot(h_ref[...], w_ref[...]).astype(z_ref.dtype)


def _inproj(x, g, w_bf, tm, tn, z_dtype):
    n = x.shape[0]
    return pl.pallas_call(
        _inproj_kernel,
        grid=(n // tm, D_IN // tn),
        in_specs=[
            pl.BlockSpec((tm, D_MODEL), lambda i, j: (i, 0)),
            pl.BlockSpec((1, D_MODEL), lambda i, j: (0, 0)),
            pl.BlockSpec((D_MODEL, tn), lambda i, j: (0, j)),
        ],
        out_specs=pl.BlockSpec((tm, tn), lambda i, j: (i, j)),
        out_shape=jax.ShapeDtypeStruct((n, D_IN), z_dtype),
        scratch_shapes=[pltpu.VMEM((tm, D_MODEL), _BF)],
        compiler_params=_cparams(("parallel", "arbitrary")),
        name="inproj",
    )(x, g.reshape(1, D_MODEL), w_bf)


def _hgrn_kernel(q_ref, f_ref, i_ref, g_ref, lb_ref, on_ref, s0_ref, a_ref, s_out_ref,
                 st_scr, b_scr, k_scr, v_scr, *, chunk, cps, sub, t_valid):
    ci = pl.program_id(1)

    @pl.when(ci == 0)
    def _():
        st_scr[...] = s0_ref[0]

    row = lax.broadcasted_iota(jnp.int32, (chunk, 1), 0)
    row8 = lax.broadcasted_iota(jnp.int32, (SUBLANES, 1), 0)
    tril = (lax.broadcasted_iota(jnp.int32, (chunk, chunk), 0)
            >= lax.broadcasted_iota(jnp.int32, (chunk, chunk), 1)).astype(_F32)
    n_sub = chunk // sub
    heads = range(HG_HEADS)
    lanes = [slice(h * HG_DK, (h + 1) * HG_DK) for h in heads]
    units = [(h, cc) for cc in range(cps) for h in heads]
    q_u, k_u, vbf_u, b_u = {}, {}, {}, {}

    for h, cc in units:
        rc, sl, hc = slice(cc * chunk, (cc + 1) * chunk), lanes[h], h * cps + cc
        qr = q_ref[rc, sl].astype(_F32)
        fr = f_ref[rc, sl].astype(_F32)
        v = i_ref[rc, sl].astype(_F32)
        lb = lb_ref[:, sl]
        fval = lb + (1.0 - lb) * _sigmoid(fr)
        log2f = jnp.log2(jnp.maximum(fval, F_FLOOR))
        k = (1.0 - lb) * _sigmoid(-fr)
        if t_valid < chunk:
            valid = row < t_valid
            log2f = jnp.where(valid, log2f, 0.0)
            k = jnp.where(valid, k, 0.0)
        b = _dot(tril, log2f, precision=_HIGHEST)
        b_scr[hc] = b
        k_scr[hc] = k
        v_scr[hc] = v
        q_u[h, cc], k_u[h, cc], vbf_u[h, cc], b_u[h, cc] = qr * _sigmoid(qr), k, v.astype(_BF), b

    st = [st_scr[h] for h in heads]
    acc = {}
    for cc in range(cps):
        for h in heads:
            q, k, v_bf, b = q_u[h, cc], k_u[h, cc], vbf_u[h, cc], b_u[h, cc]
            o = _dot_nt((q * jnp.exp2(b)).astype(_BF), st[h].astype(_BF))
            b_last = b[chunk - 1:chunk]
            k_end = (k * jnp.exp2(b_last - b)).astype(_BF)
            st[h] = st[h] * jnp.exp2(b_last) + _dot_tn(v_bf, k_end)
            for si in range(n_sub):
                r0 = si * sub
                o_s = o[r0:r0 + sub]
                if si > 0:
                    ref = b[r0 - 1:r0]
                    q_dec = (q[r0:r0 + sub] * jnp.exp2(b[r0:r0 + sub] - ref)).astype(_BF)
                    k_dec = (k[:r0] * jnp.exp2(ref - b[:r0])).astype(_BF)
                    att = _dot_nt(q_dec, k_dec)
                    o_s = o_s + _dot(att.astype(_BF), v_bf[:r0])
                for g0 in range(0, sub, SUBLANES):
                    acc[h, cc, r0 + g0] = o_s[g0:g0 + SUBLANES]
    for h in heads:
        st_scr[h] = st[h]

    for si in range(n_sub):
        r0 = si * sub
        for g0 in range(0, sub, SUBLANES):
            rg = slice(r0 + g0, r0 + g0 + SUBLANES)
            for s in range(g0 + SUBLANES):
                rs = slice(r0 + s, r0 + s + 1)
                for h, cc in units:
                    hc = h * cps + cc
                    d = b_u[h, cc][rg] - b_scr[hc, rs, :]
                    if s > g0:
                        d = jnp.where(row8 >= s - g0, d, NEG)
                    p = q_u[h, cc][rg] * jnp.exp2(d) * k_scr[hc, rs, :]
                    acc[h, cc, r0 + g0] = acc[h, cc, r0 + g0] + jnp.sum(p, axis=-1, keepdims=True) * v_scr[hc, rs, :]

    for h, cc in units:
        rc, sl = slice(cc * chunk, (cc + 1) * chunk), lanes[h]
        o = jnp.concatenate([acc[h, cc, r] for r in range(0, chunk, SUBLANES)], axis=0) if chunk > SUBLANES \
            else acc[h, cc, 0]
        gr = g_ref[rc, sl].astype(_F32)
        ms = jnp.mean(o * o, axis=-1, keepdims=True)
        a_ref[rc, sl] = o * lax.rsqrt(ms + EPS) * on_ref[...] * (gr * _sigmoid(gr))

    @pl.when(ci == pl.num_programs(1) - 1)
    def _():
        s_out_ref[0] = st_scr[...]


def _hgrn(z, lb, onorm, s0_t, n_batch, t_pad, chunk, sub, t_valid):
    cps = HG_CHUNKS_PER_STEP
    rows = cps * chunk
    n_chunks = t_pad // rows
    col = lambda c: pl.BlockSpec((rows, HG_WIDTH), lambda b, ci, c=c: (b * n_chunks + ci, c))
    state_spec = pl.BlockSpec((1, HG_HEADS, HG_DV, HG_DK), lambda b, ci: (b, 0, 0, 0))
    kern = functools.partial(_hgrn_kernel, chunk=chunk, cps=cps, sub=sub, t_valid=t_valid)
    return pl.pallas_call(
        kern,
        grid=(n_batch, n_chunks),
        in_specs=[col(COL_QA), col(COL_FA), col(COL_IA), col(COL_GA),
                  pl.BlockSpec((1, HG_WIDTH), lambda b, ci: (0, 0)),
                  pl.BlockSpec((1, HG_DV), lambda b, ci: (0, 0)),
                  state_spec],
        out_specs=[pl.BlockSpec((rows, HG_WIDTH), lambda b, ci: (b * n_chunks + ci, 0)), state_spec],
        out_shape=[jax.ShapeDtypeStruct((n_batch * t_pad, HG_WIDTH), _F32),
                   jax.ShapeDtypeStruct((n_batch, HG_HEADS, HG_DV, HG_DK), _F32)],
        scratch_shapes=[pltpu.VMEM((HG_HEADS, HG_DV, HG_DK), _F32),
                        pltpu.VMEM((HG_HEADS * cps, chunk, HG_DK), _F32),
                        pltpu.VMEM((HG_HEADS * cps, chunk, HG_DK), _F32),
                        pltpu.VMEM((HG_HEADS * cps, chunk, HG_DV), _F32)],
        compiler_params=_cparams(("parallel", "arbitrary")),
        name="hgrn",
    )(z, z, z, z, lb.reshape(1, HG_WIDTH), onorm.reshape(1, HG_DV), s0_t)


def _norm_rope(x, w, cos, sin, seg):
    lane = lax.broadcasted_iota(jnp.int32, (1, LANES), 1)
    first_half = (lane % MB_HEAD_DIM) < (MB_HEAD_DIM // 2)
    sq = x * x
    hi = sq.astype(_BF)
    lo = (sq - hi.astype(_F32)).astype(_BF)
    ss = _dot(hi, seg) + _dot(lo, seg)
    xn = x * lax.rsqrt(ss * (1.0 / MB_HEAD_DIM) + EPS) * w
    outs = []
    for g in range(MB_WIDTH // LANES):
        xg = xn[:, g * LANES:(g + 1) * LANES]
        rot = jnp.where(first_half, pltpu.roll(xg, LANES - MB_HEAD_DIM // 2, 1),
                        pltpu.roll(xg, MB_HEAD_DIM // 2, 1))
        outs.append(xg * cos + rot * sin)
    return outs


def _mb_prep_kernel(q_ref, k_ref, v_ref, cos_ref, sin_ref, qn_ref, kn_ref, seg_ref,
                    qo_ref, ko_ref, vo_ref):
    q_g = _norm_rope(q_ref[...].astype(_F32), qn_ref[...], cos_ref[...], sin_ref[...], seg_ref[...])
    k_g = _norm_rope(k_ref[...].astype(_F32), kn_ref[...], cos_ref[...], sin_ref[...], seg_ref[...])
    v = v_ref[...].astype(_F32)
    for h in range(MB_HEADS):
        g, off = divmod(h * MB_HEAD_DIM, LANES)
        qo_ref[0, h] = q_g[g][:, off:off + MB_HEAD_DIM]
        ko_ref[0, h] = k_g[g][:, off:off + MB_HEAD_DIM]
        vo_ref[0, h] = v[:, h * MB_HEAD_DIM:(h + 1) * MB_HEAD_DIM]


def _mb_prep_t_kernel(q_ref, k_ref, v_ref, cos_ref, sin_ref, qn_ref, kn_ref, seg_ref, *rest):
    qo_ref, ko_ref, vo_ref = rest[-3:]
    q_g = _norm_rope(q_ref[...].astype(_F32), qn_ref[...], cos_ref[...], sin_ref[...], seg_ref[...])
    k_g = _norm_rope(k_ref[...].astype(_F32), kn_ref[...], cos_ref[...], sin_ref[...], seg_ref[...])
    v = v_ref[...].astype(_F32)
    for g in range(MB_WIDTH // LANES):
        q_t = q_g[g].T
        k_t = k_g[g].T
        v_t = v[:, g * LANES:(g + 1) * LANES].T
        for hh in range(HEADS_PER_STEP):
            h = g * HEADS_PER_STEP + hh
            rows = slice(hh * MB_HEAD_DIM, (hh + 1) * MB_HEAD_DIM)
            qo_ref[0, h] = q_t[rows]
            ko_ref[0, 0, h] = k_t[rows]
            vo_ref[0, 0, h] = v_t[rows]


def _prep_common_specs(n_t, tm):
    col = lambda c: pl.BlockSpec((tm, MB_WIDTH), lambda b, ti, c=c: (b * n_t + ti, c))
    tab = pl.BlockSpec((tm, LANES), lambda b, ti: (ti, 0))
    vec = pl.BlockSpec((1, MB_WIDTH), lambda b, ti: (0, 0))
    return [col(COL_QB), col(COL_KB), col(COL_VB), tab, tab, vec, vec,
            pl.BlockSpec((MB_WIDTH, MB_WIDTH), lambda b, ti: (0, 0))]


def _mb_prep(z, cos_t, sin_t, qn, kn, seg, n_batch, t_pad, tm):
    n_t = t_pad // tm
    out = pl.BlockSpec((1, MB_HEADS, tm, MB_HEAD_DIM), lambda b, ti: (b, 0, ti, 0))
    shp = jax.ShapeDtypeStruct((n_batch, MB_HEADS, t_pad, MB_HEAD_DIM), _F32)
    return pl.pallas_call(
        _mb_prep_kernel,
        grid=(n_batch, n_t),
        in_specs=_prep_common_specs(n_t, tm),
        out_specs=[out, out, out],
        out_shape=[shp, shp, shp],
        compiler_params=_cparams(("parallel", "parallel")),
        name="mb_prep",
    )(z, z, z, cos_t, sin_t, jnp.tile(qn, MB_HEADS).reshape(1, MB_WIDTH),
      jnp.tile(kn, MB_HEADS).reshape(1, MB_WIDTH), seg)


def _mb_prep_t(z, cos_t, sin_t, qn, kn, seg, n_batch, t, tm, layer, depth, kv_prev):
    n_t = t // tm
    q_spec = pl.BlockSpec((1, MB_HEADS, MB_HEAD_DIM, tm), lambda b, ti: (b, 0, 0, ti))
    kv_spec = pl.BlockSpec((1, 1, MB_HEADS, MB_HEAD_DIM, tm), lambda b, ti: (b, layer, 0, 0, ti))
    kv_shape = jax.ShapeDtypeStruct((n_batch, depth, MB_HEADS, MB_HEAD_DIM, t), _F32)
    in_specs = _prep_common_specs(n_t, tm)
    n_in = len(in_specs)
    extra, aliases = (), {}
    if kv_prev is not None:
        extra = tuple(kv_prev)
        in_specs = in_specs + [pl.BlockSpec(memory_space=pl.ANY)] * 2
        aliases = {n_in: 1, n_in + 1: 2}
    return pl.pallas_call(
        _mb_prep_t_kernel,
        grid=(n_batch, n_t),
        in_specs=in_specs,
        out_specs=[q_spec, kv_spec, kv_spec],
        out_shape=[jax.ShapeDtypeStruct((n_batch, MB_HEADS, MB_HEAD_DIM, t), _F32), kv_shape, kv_shape],
        input_output_aliases=aliases,
        compiler_params=_cparams(("parallel", "parallel")),
        name="mb_prep_t",
    )(z, z, z, cos_t, sin_t, jnp.tile(qn, MB_HEADS).reshape(1, MB_WIDTH),
      jnp.tile(kn, MB_HEADS).reshape(1, MB_WIDTH), seg, *extra)


def _top3_select(gate, axis):
    n = gate.shape[axis]
    idx = lax.broadcasted_iota(jnp.int32, gate.shape, axis).astype(_F32)
    sel = jnp.zeros(gate.shape, _F32)
    picks = []
    g = gate
    for _ in range(MB_TOPK):
        m = jnp.max(g, axis=axis, keepdims=True)
        first = jnp.min(jnp.where(g == m, idx, float(n)), axis=axis, keepdims=True)
        pick = idx == first
        sel = jnp.where(pick, 1.0, sel)
        picks.append(first)
        g = jnp.where(pick, BELOW_NEG, g)
    return sel, picks


AUG_DEPTH = 2 * MB_HEAD_DIM
V_AUG = LANES
LOG2_E = 1.4426950408889634


def _mb_attn_kernel(qt_ref, kt_ref, vt_ref, o_ref, kt_aug_scr, v_aug_scr, km_scr, s_scr, *, n_blk):
    i = pl.program_id(2)
    heads = range(HEADS_PER_STEP)
    filler = n_blk
    ext_rows = AUG_DEPTH - MB_HEAD_DIM

    @pl.when(i == 0)
    def _():
        lane = lax.broadcasted_iota(jnp.int32, (MB_HEAD_DIM, LANES), 1)
        ext_row = lax.broadcasted_iota(jnp.int32, (ext_rows, MB_BLOCK), 0)
        ones_row = jnp.where(lax.broadcasted_iota(jnp.int32, (V_AUG - MB_HEAD_DIM, MB_BLOCK), 0) == 0, 1.0, 0.0)
        for h in heads:
            km_t = jnp.zeros((MB_HEAD_DIM, LANES), _F32)
            for j in range(n_blk):
                kt_blk = kt_ref[0, 0, h, :, j * MB_BLOCK:(j + 1) * MB_BLOCK]
                vt_blk = vt_ref[0, 0, h, :, j * MB_BLOCK:(j + 1) * MB_BLOCK]
                kt_aug_scr[h, j] = jnp.concatenate([kt_blk, jnp.where(ext_row == j, 1.0, 0.0)], axis=0).astype(_BF)
                v_aug_scr[h, j] = jnp.concatenate([vt_blk, ones_row], axis=0).T.astype(_BF)
                mean = jnp.sum(kt_blk, axis=1, keepdims=True) * (1.0 / MB_BLOCK)
                km_t = jnp.where(lane == j, mean, km_t)
            km_scr[h] = km_t.T[:n_blk]
            kt_aug_scr[h, filler] = jnp.concatenate(
                [jnp.zeros((MB_HEAD_DIM, MB_BLOCK), _F32), jnp.where(ext_row == filler, 1.0, 0.0)],
                axis=0).astype(_BF)
            v_aug_scr[h, filler] = jnp.zeros((MB_BLOCK, V_AUG), _BF)

    blk_row = lax.broadcasted_iota(jnp.int32, (n_blk, MB_BLOCK), 0)
    row8 = lax.broadcasted_iota(jnp.int32, (SUBLANES, MB_BLOCK), 0)
    q_aug = []
    for h in heads:
        q_t = qt_ref[0, h]
        gate = _dot(km_scr[h], q_t, precision=_HIGHEST)
        eligible = blk_row < i
        sel, _ = _top3_select(jnp.where(eligible, gate, NEG), 0)
        sel = jnp.where(jnp.logical_or(blk_row == i, eligible), jnp.where(blk_row == i, 1.0, sel), 0.0)
        bias = (sel - 1.0) * (-NEG)
        q_s = q_t * (MB_HEAD_DIM ** -0.5 * LOG2_E)
        tail = jnp.where(row8 == 0, NEG, 0.0)
        pad = jnp.zeros((ext_rows - n_blk - SUBLANES, MB_BLOCK), _F32)
        q_aug_t = jnp.concatenate([q_s, bias, tail, pad], axis=0)
        q_aug.append(q_aug_t.T.astype(_BF))

    n_pairs = (i + 1) // 2

    def pair_blocks(jj):
        j0 = 2 * jj
        return (j0, jnp.where(j0 + 1 < i, j0 + 1, filler))

    def fold_max(s):
        return jnp.maximum(s[:, :LANES], s[:, LANES:])

    query_ge_key = (lax.broadcasted_iota(jnp.int32, (MB_BLOCK, MB_BLOCK), 0)
                    >= lax.broadcasted_iota(jnp.int32, (MB_BLOCK, MB_BLOCK), 1))
    m_init = []
    for h in heads:
        s = jnp.where(query_ge_key, _dot(q_aug[h], kt_aug_scr[h, i]), NEG)
        s_scr[h, i] = s
        m_init.append(fold_max(s))

    def score_body(jj, m):
        m = list(m)
        for j in pair_blocks(jj):
            for h in heads:
                s = _dot(q_aug[h], kt_aug_scr[h, j])
                s_scr[h, j] = s
                m[h] = jnp.maximum(m[h], fold_max(s))
        return tuple(m)

    m = lax.fori_loop(0, n_pairs, score_body, tuple(m_init))
    m_b = [jnp.broadcast_to(jnp.max(m[h], axis=1, keepdims=True), (MB_BLOCK, LANES)) for h in heads]

    def weighted_values(h, j):
        s = s_scr[h, j]
        p = jnp.concatenate([jnp.exp2(s[:, :LANES] - m_b[h]), jnp.exp2(s[:, LANES:] - m_b[h])], axis=1)
        return _dot(p.astype(_BF), v_aug_scr[h, j])

    def acc_body(jj, acc):
        acc = list(acc)
        for j in pair_blocks(jj):
            for h in heads:
                acc[h] = acc[h] + weighted_values(h, j)
        return tuple(acc)

    acc = lax.fori_loop(0, n_pairs, acc_body, tuple(weighted_values(h, i) for h in heads))
    o_ref[...] = jnp.concatenate(
        [acc[h][:, :MB_HEAD_DIM] / acc[h][:, MB_HEAD_DIM:MB_HEAD_DIM + 1] for h in heads], axis=1)


def _mb_attn(q_t, k_t_all, v_t_all, layer, n_batch, t):
    n_blk = t // MB_BLOCK
    assert n_blk % SUBLANES == 0 and MB_HEAD_DIM + n_blk + SUBLANES <= AUG_DEPTH
    n_hp = MB_HEADS // HEADS_PER_STEP
    kern = functools.partial(_mb_attn_kernel, n_blk=n_blk)
    kv_spec = pl.BlockSpec((1, 1, HEADS_PER_STEP, MB_HEAD_DIM, t), lambda b, hp, i: (b, layer, hp, 0, 0))
    return pl.pallas_call(
        kern,
        grid=(n_batch, n_hp, n_blk),
        in_specs=[pl.BlockSpec((1, HEADS_PER_STEP, MB_HEAD_DIM, MB_BLOCK), lambda b, hp, i: (b, hp, 0, i)),
                  kv_spec, kv_spec],
        out_specs=pl.BlockSpec((MB_BLOCK, LANES), lambda b, hp, i: (b * n_blk + i, hp)),
        out_shape=jax.ShapeDtypeStruct((n_batch * t, MB_WIDTH), _F32),
        scratch_shapes=[pltpu.VMEM((HEADS_PER_STEP, n_blk + 1, AUG_DEPTH, MB_BLOCK), _BF),
                        pltpu.VMEM((HEADS_PER_STEP, n_blk + 1, MB_BLOCK, V_AUG), _BF),
                        pltpu.VMEM((HEADS_PER_STEP, n_blk, MB_HEAD_DIM), _F32),
                        pltpu.VMEM((HEADS_PER_STEP, n_blk + 1, MB_BLOCK, MB_BLOCK), _F32)],
        compiler_params=_cparams(("parallel", "parallel", "arbitrary")),
        name="mb_attn",
    )(q_t, k_t_all, v_t_all)


def _kmean_kernel(pt_ref, *refs):
    page_refs, out_ref = refs[:KMEAN_PAGES_PER_STEP], refs[KMEAN_PAGES_PER_STEP]
    for jb in range(KMEAN_PAGES_PER_STEP // PAGES_PER_BLOCK):
        tot = None
        for p in range(PAGES_PER_BLOCK):
            page = page_refs[jb * PAGES_PER_BLOCK + p][0, 0]
            tot = page if tot is None else tot + page
        out_ref[0, 0, 0, :, :, jb:jb + 1] = jnp.sum(tot, axis=-1, keepdims=True) * (1.0 / MB_BLOCK)


def _kmean(cache_kt, pt_flat, n_layers, n_batch, n_pages):
    n_steps = n_pages // KMEAN_PAGES_PER_STEP
    blocks_per_step = KMEAN_PAGES_PER_STEP // PAGES_PER_BLOCK

    def page_spec(p):
        return pl.BlockSpec((1, 1, MB_HEADS, MB_HEAD_DIM, PAGE_SIZE),
                            lambda l, b, c, pt, p=p: (pt[b * n_pages + c * KMEAN_PAGES_PER_STEP + p], l, 0, 0, 0))

    grid_spec = pltpu.PrefetchScalarGridSpec(
        num_scalar_prefetch=1,
        grid=(n_layers, n_batch, n_steps),
        in_specs=[page_spec(p) for p in range(KMEAN_PAGES_PER_STEP)],
        out_specs=pl.BlockSpec((1, 1, 1, MB_HEADS, MB_HEAD_DIM, blocks_per_step),
                               lambda l, b, c, pt: (l, b, c, 0, 0, 0)),
    )
    return pl.pallas_call(
        _kmean_kernel,
        grid_spec=grid_spec,
        out_shape=jax.ShapeDtypeStruct((n_layers, n_batch, n_steps, MB_HEADS, MB_HEAD_DIM, blocks_per_step), _F32),
        compiler_params=_cparams(("parallel", "parallel", "parallel")),
        name="kmean",
    )(pt_flat, *([cache_kt] * KMEAN_PAGES_PER_STEP))


GATE_PAGE_LANE0 = SUBLANES


def _gate_kernel(q_ref, km_ref, pt_ref, idx_ref, *, own):
    n_past = km_ref.shape[3]
    n_pages = pt_ref.shape[2]
    col = lax.broadcasted_iota(jnp.int32, (SAMPLE_T_PAD, n_past), 1)
    lane = lax.broadcasted_iota(jnp.int32, (SAMPLE_T_PAD, LANES), 1)
    page_col = lax.broadcasted_iota(jnp.int32, (SAMPLE_T_PAD, n_pages), 1).astype(_F32)
    pt_row = pt_ref[0].astype(_F32)
    for h in range(MB_HEADS):
        gate = _dot(q_ref[0, h], km_ref[0, h], precision=_HIGHEST)
        _, picks = _top3_select(jnp.where(col < own, gate, NEG), 1)
        out = jnp.zeros((SAMPLE_T_PAD, LANES), jnp.int32)
        for r, first in enumerate(picks):
            out = jnp.where(lane == r, first.astype(jnp.int32), out)
            for p in range(PAGES_PER_BLOCK):
                logical = first * PAGES_PER_BLOCK + p
                phys = jnp.sum(jnp.where(page_col == logical, pt_row, 0.0), axis=1, keepdims=True)
                out = jnp.where(lane == GATE_PAGE_LANE0 + r * PAGES_PER_BLOCK + p, phys.astype(jnp.int32), out)
        idx_ref[0, h] = out


def _gate(q_s, kmean_l, page_table, own):
    n_batch = q_s.shape[0]
    n_past = kmean_l.shape[3]
    n_pages = page_table.shape[1]
    return pl.pallas_call(
        functools.partial(_gate_kernel, own=own),
        grid=(n_batch,),
        in_specs=[pl.BlockSpec((1, MB_HEADS, SAMPLE_T_PAD, MB_HEAD_DIM), lambda b: (b, 0, 0, 0)),
                  pl.BlockSpec((1, MB_HEADS, MB_HEAD_DIM, n_past), lambda b: (b, 0, 0, 0)),
                  pl.BlockSpec((1, 1, n_pages), lambda b: (b, 0, 0))],
        out_specs=pl.BlockSpec((1, MB_HEADS, SAMPLE_T_PAD, LANES), lambda b: (b, 0, 0, 0)),
        out_shape=jax.ShapeDtypeStruct((n_batch, MB_HEADS, SAMPLE_T_PAD, LANES), jnp.int32),
        compiler_params=_cparams(("parallel",)),
        name="mb_gate",
    )(q_s, kmean_l, page_table.reshape(n_batch, 1, n_pages))


def _sample_attn_kernel(pg_ref, idx_ref, q_ref, kn_ref, vn_ref, *refs, t_s, own):
    n_slots = t_s * MB_TOPK * PAGES_PER_BLOCK
    k_pages, v_pages, o_ref = refs[:n_slots], refs[n_slots:2 * n_slots], refs[2 * n_slots]
    b = pl.program_id(0)
    h = pl.program_id(1)
    q = q_ref[0, 0] * (MB_HEAD_DIM ** -0.5)
    q_bf = q.astype(_BF)
    row = lax.broadcasted_iota(jnp.int32, (SAMPLE_T_PAD, 1), 0)

    scores = []
    for t in range(t_s):
        for j in range(MB_TOPK):
            blk = idx_ref[((b * MB_HEADS + h) * t_s + t) * MB_TOPK + j]
            allowed = row == jnp.where(blk < own, t, -1)
            for p in range(PAGES_PER_BLOCK):
                slot = (t * MB_TOPK + j) * PAGES_PER_BLOCK + p
                s = _dot(q_bf, k_pages[slot][0, 0, 0].astype(_BF))
                scores.append(jnp.where(allowed, s, NEG))
    kn = kn_ref[0, 0]
    vn = vn_ref[0, 0]
    own_scores = []
    for c in range(t_s):
        s = jnp.sum(q * kn[c:c + 1, :], axis=1, keepdims=True)
        own_scores.append(jnp.where(row >= c, s, NEG))

    m = own_scores[0]
    for s in own_scores[1:]:
        m = jnp.maximum(m, s)
    for s in scores:
        m = jnp.maximum(m, jnp.max(s, axis=1, keepdims=True))
    l = jnp.zeros((SAMPLE_T_PAD, 1), _F32)
    acc = jnp.zeros((SAMPLE_T_PAD, MB_HEAD_DIM), _F32)
    for c, s in enumerate(own_scores):
        p = jnp.exp(s - m)
        l = l + p
        acc = acc + p * vn[c:c + 1, :]
    for slot, s in enumerate(scores):
        p = jnp.exp(s - m)
        l = l + jnp.sum(p, axis=1, keepdims=True)
        acc = acc + _dot_nt(p.astype(_BF), v_pages[slot][0, 0, 0].astype(_BF))
    o_ref[0, 0] = acc / l


def _sample_attn(q_s, k_s, v_s, cache_kt, cache_vt, pages_flat, idx_flat, layer, t_s, own):
    n_batch = q_s.shape[0]
    new_spec = pl.BlockSpec((1, 1, SAMPLE_T_PAD, MB_HEAD_DIM), lambda b, h, pg, idx: (b, h, 0, 0))

    def page_spec(t, j, p):
        def index_map(b, h, pg, idx):
            slot = ((b * MB_HEADS + h) * t_s + t) * MB_TOPK + j
            return (pg[slot * PAGES_PER_BLOCK + p], layer, h, 0, 0)
        return pl.BlockSpec((1, 1, 1, MB_HEAD_DIM, PAGE_SIZE), index_map)

    page_specs = [page_spec(t, j, p) for t in range(t_s) for j in range(MB_TOPK) for p in range(PAGES_PER_BLOCK)]
    n_slots = len(page_specs)
    grid_spec = pltpu.PrefetchScalarGridSpec(
        num_scalar_prefetch=2,
        grid=(n_batch, MB_HEADS),
        in_specs=[new_spec, new_spec, new_spec] + page_specs + page_specs,
        out_specs=new_spec,
    )
    return pl.pallas_call(
        functools.partial(_sample_attn_kernel, t_s=t_s, own=own),
        grid_spec=grid_spec,
        out_shape=jax.ShapeDtypeStruct((n_batch, MB_HEADS, SAMPLE_T_PAD, MB_HEAD_DIM), _F32),
        compiler_params=_cparams(("parallel", "parallel")),
        name="mb_sample_attn",
    )(pages_flat, idx_flat, q_s, k_s, v_s, *([cache_kt] * n_slots), *([cache_vt] * n_slots))


def _pool_kernel(u_ref, hist_ref, w_ref, sc_ref, y_ref, hist_out_ref, ext_scr, *, tm, pos0, t_last):
    ti = pl.program_id(1)

    @pl.when(ti == 0)
    def _():
        ext_scr[0:POOL_HALO, :] = hist_ref[0]

    u = u_ref[...].astype(_F32)
    ext_scr[POOL_HALO:POOL_HALO + tm, :] = u
    pos = pos0 + ti * tm + lax.broadcasted_iota(jnp.int32, (tm, 1), 0)
    for g, w in enumerate(POOL_WINDOWS):
        sl = slice(g * POOL_GROUP_DIM, (g + 1) * POOL_GROUP_DIM)
        tot = u[:, sl]
        for d in range(1, w):
            tot = tot + ext_scr[POOL_HALO - d:POOL_HALO - d + tm, sl]
        cnt = jnp.minimum(pos + 1, w).astype(_F32)
        diff = tot / cnt - u[:, sl]
        y = _dot(diff.astype(_BF), w_ref[g])
        y_ref[:, sl] = y * sc_ref[:, sl]

    @pl.when(ti == pl.num_programs(1) - 1)
    def _():
        hist_out_ref[0] = ext_scr[t_last:t_last + POOL_HALO, :]

    @pl.when(ti < pl.num_programs(1) - 1)
    def _():
        ext_scr[0:POOL_HALO, :] = ext_scr[tm:tm + POOL_HALO, :]


def _pool(z, hist16, pool_w_bf, pool_scale, n_batch, t_pad, tm, pos0, t_valid):
    n_t = t_pad // tm
    t_last = t_valid - (n_t - 1) * tm
    hist_spec = pl.BlockSpec((1, POOL_HALO, POOL_WIDTH), lambda b, ti: (b, 0, 0))
    kern = functools.partial(_pool_kernel, tm=tm, pos0=pos0, t_last=t_last)
    return pl.pallas_call(
        kern,
        grid=(n_batch, n_t),
        in_specs=[pl.BlockSpec((tm, POOL_WIDTH), lambda b, ti: (b * n_t + ti, COL_UC)),
                  hist_spec,
                  pl.BlockSpec((POOL_GROUPS, POOL_GROUP_DIM, POOL_GROUP_DIM), lambda b, ti: (0, 0, 0)),
                  pl.BlockSpec((1, POOL_WIDTH), lambda b, ti: (0, 0))],
        out_specs=[pl.BlockSpec((tm, POOL_WIDTH), lambda b, ti: (b * n_t + ti, 0)), hist_spec],
        out_shape=[jax.ShapeDtypeStruct((n_batch * t_pad, POOL_WIDTH), _F32),
                   jax.ShapeDtypeStruct((n_batch, POOL_HALO, POOL_WIDTH), _F32)],
        scratch_shapes=[pltpu.VMEM((POOL_HALO + tm, POOL_WIDTH), _F32)],
        compiler_params=_cparams(("parallel", "arbitrary")),
        name="pool",
    )(z, hist16, pool_w_bf, pool_scale.reshape(1, POOL_WIDTH))


def _merge_kernel(x_ref, a_ref, b_ref, c_ref, g0_ref, g1_ref, g2_ref, wb_ref, wo_ref, o_ref):
    merged = None
    for br_ref, g_ref, n in ((a_ref, g0_ref, 0), (b_ref, g1_ref, 1), (c_ref, g2_ref, 2)):
        proj = _dot(br_ref[...].astype(_BF), wb_ref[n])
        term = _sigmoid(g_ref[...].astype(_F32)) * proj
        merged = term if merged is None else merged + term
    o_ref[...] = x_ref[...] + _dot(merged.astype(_BF), wo_ref[...])


def _merge(x, a, b, c, z, wb_bf, wo_bf, tm):
    n = x.shape[0]
    row = lambda w: pl.BlockSpec((tm, w), lambda i: (i, 0))
    gate = lambda k: pl.BlockSpec((tm, D_MODEL), lambda i, k=k: (i, COL_GATE_1024 + k))
    return pl.pallas_call(
        _merge_kernel,
        grid=(n // tm,),
        in_specs=[row(D_MODEL), row(BRANCH_WIDTH), row(BRANCH_WIDTH), row(BRANCH_WIDTH),
                  gate(0), gate(1), gate(2),
                  pl.BlockSpec((N_BRANCH, BRANCH_WIDTH, D_MODEL), lambda i: (0, 0, 0)),
                  pl.BlockSpec((D_MODEL, D_MODEL), lambda i: (0, 0))],
        out_specs=row(D_MODEL),
        out_shape=jax.ShapeDtypeStruct((n, D_MODEL), _F32),
        compiler_params=_cparams(("parallel",)),
        name="merge",
    )(x, a, b, c, z, z, z, wb_bf, wo_bf)


def _ffn_kernel(x_ref, g_ref, w1_ref, w2_ref, o_ref):
    x = x_ref[...]
    ms = jnp.mean(x * x, axis=-1, keepdims=True)
    h = (x * lax.rsqrt(ms + EPS) * g_ref[...]).astype(_BF)
    hid = jnp.maximum(_dot(h, w1_ref[...]), 0.0)
    hid = (hid * hid).astype(_BF)
    o_ref[...] = x + _dot(hid, w2_ref[...])


def _ffn(x, g, w1_bf, w2_bf, tm):
    n = x.shape[0]
    row = pl.BlockSpec((tm, D_MODEL), lambda i: (i, 0))
    return pl.pallas_call(
        _ffn_kernel,
        grid=(n // tm,),
        in_specs=[row, pl.BlockSpec((1, D_MODEL), lambda i: (0, 0)),
                  pl.BlockSpec((D_MODEL, D_FF), lambda i: (0, 0)),
                  pl.BlockSpec((D_FF, D_MODEL), lambda i: (0, 0))],
        out_specs=row,
        out_shape=jax.ShapeDtypeStruct((n, D_MODEL), _F32),
        compiler_params=_cparams(("parallel",)),
        name="ffn",
    )(x, g.reshape(1, D_MODEL), w1_bf, w2_bf)


def _rope_tables(pos):
    half = MB_HEAD_DIM // 2
    inv = ROPE_THETA ** (-jnp.arange(half, dtype=_F32) / half)
    ang = pos.astype(_F32)[:, None] * inv[None, :]
    cos, sin = jnp.cos(ang), jnp.sin(ang)
    cos_t = jnp.concatenate([cos, cos, cos, cos], axis=1)
    sin_t = jnp.concatenate([-sin, sin, -sin, sin], axis=1)
    return cos_t, sin_t


def _tile_rows(n, cap):
    t = cap
    while n % t:
        t //= 2
    return t


def _layer(x, *, n_batch, t_pad, t_valid, pos0, chunk, sub, lb, s0_t, hist16, moba, w):
    n = x.shape[0]
    z_dtype = _BF if t_pad % (2 * SUBLANES) == 0 else _F32
    z = _inproj(x, w["norm_mix"], w["w_in"], _tile_rows(n, 1024), D_IN // 4, z_dtype)
    a_out, s_new_t = _hgrn(z, lb, w["hg_onorm"], s0_t, n_batch, t_pad, chunk, sub, t_valid)
    b_out, kv = moba(z)
    c_out, hist_new = _pool(z, hist16, w["pool_w"], w["pool_scale"], n_batch, t_pad, _tile_rows(t_pad, 512),
                            pos0, t_valid)
    x1 = _merge(x, a_out, b_out, c_out, z, w["w_branch"], w["w_out"], _tile_rows(n, 512))
    x2 = _ffn(x1, w["norm_ffn"], w["w_ff1"], w["w_ff2"], _tile_rows(n, 256))
    return x2, kv, s_new_t, hist_new


def kernel(x_prompt, x_sample, cache_k, cache_v, state_hgrn, state_pool, page_table, norm_mix, w_in, hg_lb,
           hg_onorm, mb_qnorm, mb_knorm, pool_w, pool_scale, w_branch, w_out, norm_ffn, w_ff1, w_ff2):
    n_b, t_p, _ = x_prompt.shape
    n_db, t_s, _ = x_sample.shape
    depth = w_in.shape[0]
    n_pages = page_table.shape[1]
    past_len = n_pages * PAGE_SIZE
    assert t_p % MB_BLOCK == 0 and t_p % HG_CHUNK == 0
    assert past_len % MB_BLOCK == 0 and t_s <= SAMPLE_T_PAD and n_pages % KMEAN_PAGES_PER_STEP == 0
    own_s = past_len // MB_BLOCK

    lb_soft = jax.nn.softmax(hg_lb.astype(_F32), axis=0)
    lb_all = jnp.cumsum(lb_soft, axis=0) - lb_soft[0:1]

    seg = (jnp.arange(MB_WIDTH)[:, None] // MB_HEAD_DIM == jnp.arange(MB_WIDTH)[None, :] // MB_HEAD_DIM).astype(_BF)
    cos_p, sin_p = _rope_tables(jnp.arange(t_p, dtype=jnp.int32))
    cos_s, sin_s = _rope_tables(past_len + jnp.arange(SAMPLE_T_PAD, dtype=jnp.int32))

    cache_kt = jnp.swapaxes(cache_k, 3, 4)
    cache_vt = jnp.swapaxes(cache_v, 3, 4)
    pt_flat = page_table.reshape(-1).astype(jnp.int32)
    kmean_all = _kmean(cache_kt, pt_flat, depth, n_db, n_pages)
    kmean_all = jnp.transpose(kmean_all, (0, 1, 3, 4, 2, 5)).reshape(depth, n_db, MB_HEADS, MB_HEAD_DIM, own_s)

    xp = x_prompt.reshape(n_b * t_p, D_MODEL)
    xs = jnp.pad(x_sample, ((0, 0), (0, SAMPLE_T_PAD - t_s), (0, 0))).reshape(n_db * SAMPLE_T_PAD, D_MODEL)

    s0_p = jnp.zeros((n_b, HG_HEADS, HG_DV, HG_DK), _F32)
    hist_p = jnp.zeros((n_b, POOL_HALO, POOL_WIDTH), _F32)

    kv_p = None
    ks_l, vs_l, sp_l, ss_l, hp_l, hs_l = [], [], [], [], [], []
    for l in range(depth):
        w = dict(norm_mix=norm_mix[l], w_in=w_in[l].astype(_BF), hg_onorm=hg_onorm[l], mb_qnorm=mb_qnorm[l],
                 mb_knorm=mb_knorm[l], pool_w=pool_w[l].astype(_BF), pool_scale=pool_scale[l],
                 w_branch=w_branch[l].astype(_BF), w_out=w_out[l].astype(_BF), norm_ffn=norm_ffn[l],
                 w_ff1=w_ff1[l].astype(_BF), w_ff2=w_ff2[l].astype(_BF))

        def moba_p(z, l=l, w=w, kv_prev=kv_p):
            q_t, k_t_all, v_t_all = _mb_prep_t(z, cos_p, sin_p, w["mb_qnorm"], w["mb_knorm"], seg, n_b, t_p,
                                               _tile_rows(t_p, 512), l, depth, kv_prev)
            return _mb_attn(q_t, k_t_all, v_t_all, l, n_b, t_p), (k_t_all, v_t_all)

        xp, kv_p, sp_t, hp = _layer(
            xp, n_batch=n_b, t_pad=t_p, t_valid=t_p, pos0=0, chunk=HG_CHUNK, sub=HG_SUB, lb=lb_all[l],
            s0_t=s0_p, hist16=hist_p, moba=moba_p, w=w)

        def moba_s(z, l=l, w=w):
            q, k, v = _mb_prep(z, cos_s, sin_s, w["mb_qnorm"], w["mb_knorm"], seg, n_db, SAMPLE_T_PAD, SAMPLE_T_PAD)
            idx = _gate(q, kmean_all[l], page_table, own_s)
            idx_flat = idx[:, :, :t_s, :MB_TOPK].reshape(-1)
            pages_flat = idx[:, :, :t_s, GATE_PAGE_LANE0:GATE_PAGE_LANE0 + MB_TOPK * PAGES_PER_BLOCK].reshape(-1)
            o = _sample_attn(q, k, v, cache_kt, cache_vt, pages_flat, idx_flat, l, t_s, own_s)
            return jnp.transpose(o, (0, 2, 1, 3)).reshape(n_db * SAMPLE_T_PAD, MB_WIDTH), (k, v)

        s0_s = jnp.swapaxes(state_hgrn[l], -1, -2)
        hist_s = jnp.pad(state_pool[l], ((0, 0), (POOL_HALO - POOL_HIST, 0), (0, 0)))
        xs, (ks, vs), ss_t, hs = _layer(
            xs, n_batch=n_db, t_pad=SAMPLE_T_PAD, t_valid=t_s, pos0=past_len, chunk=SAMPLE_T_PAD,
            sub=SAMPLE_T_PAD, lb=lb_all[l], s0_t=s0_s, hist16=hist_s, moba=moba_s, w=w)

        ks_l.append(ks[:, :, :t_s]); vs_l.append(vs[:, :, :t_s])
        sp_l.append(jnp.swapaxes(sp_t, -1, -2)); ss_l.append(jnp.swapaxes(ss_t, -1, -2))
        hp_l.append(hp[:, POOL_HALO - POOL_HIST:]); hs_l.append(hs[:, POOL_HALO - POOL_HIST:])

    y_p = xp.reshape(n_b, t_p, D_MODEL)
    y_s = xs.reshape(n_db, SAMPLE_T_PAD, D_MODEL)[:, :t_s]
    k_p = jnp.swapaxes(kv_p[0], 3, 4)
    v_p = jnp.swapaxes(kv_p[1], 3, 4)
    return (y_p, y_s, k_p, v_p, jnp.stack(ks_l, axis=1), jnp.stack(vs_l, axis=1),
            jnp.stack(sp_l, axis=0), jnp.stack(ss_l, axis=0), jnp.stack(hp_l, axis=0), jnp.stack(hs_l, axis=0))
```

```python
import functools

import jax
import jax.numpy as jnp
from jax import lax
from jax.experimental import pallas as pl
from jax.experimental.pallas import tpu as pltpu

D_MODEL = 1024
PAGE_SIZE = 128
HG_HEADS = 4
HG_DK = 128
HG_DV = 128
HG_WIDTH = HG_HEADS * HG_DV
HG_CHUNK = 64
HG_SUB = 16
HG_CHUNKS_PER_STEP = 1
MB_HEADS = 8
MB_HEAD_DIM = 64
MB_WIDTH = MB_HEADS * MB_HEAD_DIM
MB_BLOCK = 256
MB_TOPK = 3
ROPE_THETA = 10000.0
POOL_WINDOWS = (2, 4, 8, 16)
POOL_GROUPS = 4
POOL_GROUP_DIM = 128
POOL_WIDTH = POOL_GROUPS * POOL_GROUP_DIM
POOL_HIST = 15
POOL_HALO = 16
N_BRANCH = 3
BRANCH_WIDTH = 512
D_FF = 4 * D_MODEL
EPS = 1e-6
NEG = -1e30
BELOW_NEG = -3e38
F_FLOOR = 1e-30
D_IN = 4 * HG_WIDTH + 3 * MB_WIDTH + POOL_WIDTH + N_BRANCH * D_MODEL
COL_QA, COL_FA, COL_IA, COL_GA, COL_QB, COL_KB, COL_VB, COL_UC = range(8)
COL_GATE_1024 = 4

LANES = 128
SUBLANES = 8
SAMPLE_T_PAD = SUBLANES
VMEM_LIMIT = 56 * 1024 * 1024
PAGES_PER_BLOCK = MB_BLOCK // PAGE_SIZE
KMEAN_PAGES_PER_STEP = 32
HEADS_PER_STEP = LANES // MB_HEAD_DIM
ATTN_HEADS = 4

_BF = jnp.bfloat16
_F32 = jnp.float32
_HIGHEST = lax.Precision.HIGHEST


def _cparams(sem, vmem=VMEM_LIMIT):
    return pltpu.CompilerParams(dimension_semantics=sem, vmem_limit_bytes=vmem)


def _dot(a, b, precision=None):
    return jnp.dot(a, b, preferred_element_type=_F32, precision=precision)


def _dot_nt(a, b, precision=None):
    return lax.dot_general(a, b, (((1,), (1,)), ((), ())), preferred_element_type=_F32, precision=precision)


def _dot_tn(a, b):
    return lax.dot_general(a, b, (((0,), (0,)), ((), ())), preferred_element_type=_F32)


def _sigmoid(x):
    return 1.0 / (1.0 + jnp.exp(-x))


def _inproj_kernel(x_ref, g_ref, w_ref, z_ref, h_ref):
    @pl.when(pl.program_id(1) == 0)
    def _():
        x = x_ref[...]
        ms = jnp.mean(x * x, axis=-1, keepdims=True)
        h_ref[...] = (x * lax.rsqrt(ms + EPS) * g_ref[...]).astype(_BF)

    z_ref[...] = _dot(h_ref[...], w_ref[...]).astype(z_ref.dtype)


def _inproj(x, g, w_bf, tm, tn, z_dtype):
    n = x.shape[0]
    return pl.pallas_call(
        _inproj_kernel,
        grid=(n // tm, D_IN // tn),
        in_specs=[
            pl.BlockSpec((tm, D_MODEL), lambda i, j: (i, 0)),
            pl.BlockSpec((1, D_MODEL), lambda i, j: (0, 0)),
            pl.BlockSpec((D_MODEL, tn), lambda i, j: (0, j)),
        ],
        out_specs=pl.BlockSpec((tm, tn), lambda i, j: (i, j)),
        out_shape=jax.ShapeDtypeStruct((n, D_IN), z_dtype),
        scratch_shapes=[pltpu.VMEM((tm, D_MODEL), _BF)],
        compiler_params=_cparams(("parallel", "arbitrary")),
        name="inproj",
    )(x, g.reshape(1, D_MODEL), w_bf)


def _hgrn_kernel(q_ref, f_ref, i_ref, g_ref, lb_ref, on_ref, s0_ref, a_ref, s_out_ref,
                 st_scr, b_scr, k_scr, v_scr, *, chunk, cps, sub, t_valid):
    ci = pl.program_id(1)

    @pl.when(ci == 0)
    def _():
        st_scr[...] = s0_ref[0]

    row = lax.broadcasted_iota(jnp.int32, (chunk, 1), 0)
    row8 = lax.broadcasted_iota(jnp.int32, (SUBLANES, 1), 0)
    tril = (lax.broadcasted_iota(jnp.int32, (chunk, chunk), 0)
            >= lax.broadcasted_iota(jnp.int32, (chunk, chunk), 1)).astype(_F32)
    n_sub = chunk // sub
    heads = range(HG_HEADS)
    lanes = [slice(h * HG_DK, (h + 1) * HG_DK) for h in heads]
    units = [(h, cc) for cc in range(cps) for h in heads]
    q_u, k_u, vbf_u, b_u = {}, {}, {}, {}

    for h, cc in units:
        rc, sl, hc = slice(cc * chunk, (cc + 1) * chunk), lanes[h], h * cps + cc
        qr = q_ref[rc, sl].astype(_F32)
        fr = f_ref[rc, sl].astype(_F32)
        v = i_ref[rc, sl].astype(_F32)
        lb = lb_ref[:, sl]
        fval = lb + (1.0 - lb) * _sigmoid(fr)
        log2f = jnp.log2(jnp.maximum(fval, F_FLOOR))
        k = (1.0 - lb) * _sigmoid(-fr)
        if t_valid < chunk:
            valid = row < t_valid
            log2f = jnp.where(valid, log2f, 0.0)
            k = jnp.where(valid, k, 0.0)
        b = _dot(tril, log2f, precision=_HIGHEST)
        b_scr[hc] = b
        k_scr[hc] = k
        v_scr[hc] = v
        q_u[h, cc], k_u[h, cc], vbf_u[h, cc], b_u[h, cc] = qr * _sigmoid(qr), k, v.astype(_BF), b

    st = [st_scr[h] for h in heads]
    acc = {}
    for cc in range(cps):
        for h in heads:
            q, k, v_bf, b = q_u[h, cc], k_u[h, cc], vbf_u[h, cc], b_u[h, cc]
            o = _dot_nt((q * jnp.exp2(b)).astype(_BF), st[h].astype(_BF))
            b_last = b[chunk - 1:chunk]
            k_end = (k * jnp.exp2(b_last - b)).astype(_BF)
            st[h] = st[h] * jnp.exp2(b_last) + _dot_tn(v_bf, k_end)
            for si in range(n_sub):
                r0 = si * sub
                o_s = o[r0:r0 + sub]
                if si > 0:
                    ref = b[r0 - 1:r0]
                    q_dec = (q[r0:r0 + sub] * jnp.exp2(b[r0:r0 + sub] - ref)).astype(_BF)
                    k_dec = (k[:r0] * jnp.exp2(ref - b[:r0])).astype(_BF)
                    att = _dot_nt(q_dec, k_dec)
                    o_s = o_s + _dot(att.astype(_BF), v_bf[:r0])
                for g0 in range(0, sub, SUBLANES):
                    acc[h, cc, r0 + g0] = o_s[g0:g0 + SUBLANES]
    for h in heads:
        st_scr[h] = st[h]

    for si in range(n_sub):
        r0 = si * sub
        for g0 in range(0, sub, SUBLANES):
            rg = slice(r0 + g0, r0 + g0 + SUBLANES)
            for s in range(g0 + SUBLANES):
                rs = slice(r0 + s, r0 + s + 1)
                for h, cc in units:
                    hc = h * cps + cc
                    d = b_u[h, cc][rg] - b_scr[hc, rs, :]
                    if s > g0:
                        d = jnp.where(row8 >= s - g0, d, NEG)
                    p = q_u[h, cc][rg] * jnp.exp2(d) * k_scr[hc, rs, :]
                    acc[h, cc, r0 + g0] = acc[h, cc, r0 + g0] + jnp.sum(p, axis=-1, keepdims=True) * v_scr[hc, rs, :]

    for h, cc in units:
        rc, sl = slice(cc * chunk, (cc + 1) * chunk), lanes[h]
        o = jnp.concatenate([acc[h, cc, r] for r in range(0, chunk, SUBLANES)], axis=0) if chunk > SUBLANES \
            else acc[h, cc, 0]
        gr = g_ref[rc, sl].astype(_F32)
        ms = jnp.mean(o * o, axis=-1, keepdims=True)
        a_ref[rc, sl] = o * lax.rsqrt(ms + EPS) * on_ref[...] * (gr * _sigmoid(gr))

    @pl.when(ci == pl.num_programs(1) - 1)
    def _():
        s_out_ref[0] = st_scr[...]


def _hgrn(z, lb, onorm, s0_t, n_batch, t_pad, chunk, sub, t_valid):
    cps = HG_CHUNKS_PER_STEP
    rows = cps * chunk
    n_chunks = t_pad // rows
    col = lambda c: pl.BlockSpec((rows, HG_WIDTH), lambda b, ci, c=c: (b * n_chunks + ci, c))
    state_spec = pl.BlockSpec((1, HG_HEADS, HG_DV, HG_DK), lambda b, ci: (b, 0, 0, 0))
    kern = functools.partial(_hgrn_kernel, chunk=chunk, cps=cps, sub=sub, t_valid=t_valid)
    return pl.pallas_call(
        kern,
        grid=(n_batch, n_chunks),
        in_specs=[col(COL_QA), col(COL_FA), col(COL_IA), col(COL_GA),
                  pl.BlockSpec((1, HG_WIDTH), lambda b, ci: (0, 0)),
                  pl.BlockSpec((1, HG_DV), lambda b, ci: (0, 0)),
                  state_spec],
        out_specs=[pl.BlockSpec((rows, HG_WIDTH), lambda b, ci: (b * n_chunks + ci, 0)), state_spec],
        out_shape=[jax.ShapeDtypeStruct((n_batch * t_pad, HG_WIDTH), _F32),
                   jax.ShapeDtypeStruct((n_batch, HG_HEADS, HG_DV, HG_DK), _F32)],
        scratch_shapes=[pltpu.VMEM((HG_HEADS, HG_DV, HG_DK), _F32),
                        pltpu.VMEM((HG_HEADS * cps, chunk, HG_DK), _F32),
                        pltpu.VMEM((HG_HEADS * cps, chunk, HG_DK), _F32),
                        pltpu.VMEM((HG_HEADS * cps, chunk, HG_DV), _F32)],
        compiler_params=_cparams(("parallel", "arbitrary")),
        name="hgrn",
    )(z, z, z, z, lb.reshape(1, HG_WIDTH), onorm.reshape(1, HG_DV), s0_t)


def _norm_rope(x, w, cos, sin, seg):
    lane = lax.broadcasted_iota(jnp.int32, (1, LANES), 1)
    first_half = (lane % MB_HEAD_DIM) < (MB_HEAD_DIM // 2)
    sq = x * x
    hi = sq.astype(_BF)
    lo = (sq - hi.astype(_F32)).astype(_BF)
    ss = _dot(hi, seg) + _dot(lo, seg)
    xn = x * lax.rsqrt(ss * (1.0 / MB_HEAD_DIM) + EPS) * w
    outs = []
    for g in range(MB_WIDTH // LANES):
        xg = xn[:, g * LANES:(g + 1) * LANES]
        rot = jnp.where(first_half, pltpu.roll(xg, LANES - MB_HEAD_DIM // 2, 1),
                        pltpu.roll(xg, MB_HEAD_DIM // 2, 1))
        outs.append(xg * cos + rot * sin)
    return outs


def _mb_prep_kernel(q_ref, k_ref, v_ref, cos_ref, sin_ref, qn_ref, kn_ref, seg_ref,
                    qo_ref, ko_ref, vo_ref):
    q_g = _norm_rope(q_ref[...].astype(_F32), qn_ref[...], cos_ref[...], sin_ref[...], seg_ref[...])
    k_g = _norm_rope(k_ref[...].astype(_F32), kn_ref[...], cos_ref[...], sin_ref[...], seg_ref[...])
    v = v_ref[...].astype(_F32)
    for h in range(MB_HEADS):
        g, off = divmod(h * MB_HEAD_DIM, LANES)
        qo_ref[0, h] = q_g[g][:, off:off + MB_HEAD_DIM]
        ko_ref[0, h] = k_g[g][:, off:off + MB_HEAD_DIM]
        vo_ref[0, h] = v[:, h * MB_HEAD_DIM:(h + 1) * MB_HEAD_DIM]


def _mb_prep_t_kernel(q_ref, k_ref, v_ref, cos_ref, sin_ref, qn_ref, kn_ref, seg_ref, *rest):
    qo_ref, ko_ref, vo_ref = rest[-3:]
    q_g = _norm_rope(q_ref[...].astype(_F32), qn_ref[...], cos_ref[...], sin_ref[...], seg_ref[...])
    k_g = _norm_rope(k_ref[...].astype(_F32), kn_ref[...], cos_ref[...], sin_ref[...], seg_ref[...])
    v = v_ref[...].astype(_F32)
    for g in range(MB_WIDTH // LANES):
        q_t = q_g[g].T
        k_t = k_g[g].T
        v_t = v[:, g * LANES:(g + 1) * LANES].T
        for hh in range(HEADS_PER_STEP):
            h = g * HEADS_PER_STEP + hh
            rows = slice(hh * MB_HEAD_DIM, (hh + 1) * MB_HEAD_DIM)
            qo_ref[0, h] = q_t[rows]
            ko_ref[0, 0, h] = k_t[rows]
            vo_ref[0, 0, h] = v_t[rows]


def _prep_common_specs(n_t, tm):
    col = lambda c: pl.BlockSpec((tm, MB_WIDTH), lambda b, ti, c=c: (b * n_t + ti, c))
    tab = pl.BlockSpec((tm, LANES), lambda b, ti: (ti, 0))
    vec = pl.BlockSpec((1, MB_WIDTH), lambda b, ti: (0, 0))
    return [col(COL_QB), col(COL_KB), col(COL_VB), tab, tab, vec, vec,
            pl.BlockSpec((MB_WIDTH, MB_WIDTH), lambda b, ti: (0, 0))]


def _mb_prep(z, cos_t, sin_t, qn, kn, seg, n_batch, t_pad, tm):
    n_t = t_pad // tm
    out = pl.BlockSpec((1, MB_HEADS, tm, MB_HEAD_DIM), lambda b, ti: (b, 0, ti, 0))
    shp = jax.ShapeDtypeStruct((n_batch, MB_HEADS, t_pad, MB_HEAD_DIM), _F32)
    return pl.pallas_call(
        _mb_prep_kernel,
        grid=(n_batch, n_t),
        in_specs=_prep_common_specs(n_t, tm),
        out_specs=[out, out, out],
        out_shape=[shp, shp, shp],
        compiler_params=_cparams(("parallel", "parallel")),
        name="mb_prep",
    )(z, z, z, cos_t, sin_t, jnp.tile(qn, MB_HEADS).reshape(1, MB_WIDTH),
      jnp.tile(kn, MB_HEADS).reshape(1, MB_WIDTH), seg)


def _mb_prep_t(z, cos_t, sin_t, qn, kn, seg, n_batch, t, tm, layer, depth, kv_prev):
    n_t = t // tm
    q_spec = pl.BlockSpec((1, MB_HEADS, MB_HEAD_DIM, tm), lambda b, ti: (b, 0, 0, ti))
    kv_spec = pl.BlockSpec((1, 1, MB_HEADS, MB_HEAD_DIM, tm), lambda b, ti: (b, layer, 0, 0, ti))
    kv_shape = jax.ShapeDtypeStruct((n_batch, depth, MB_HEADS, MB_HEAD_DIM, t), _F32)
    in_specs = _prep_common_specs(n_t, tm)
    n_in = len(in_specs)
    extra, aliases = (), {}
    if kv_prev is not None:
        extra = tuple(kv_prev)
        in_specs = in_specs + [pl.BlockSpec(memory_space=pl.ANY)] * 2
        aliases = {n_in: 1, n_in + 1: 2}
    return pl.pallas_call(
        _mb_prep_t_kernel,
        grid=(n_batch, n_t),
        in_specs=in_specs,
        out_specs=[q_spec, kv_spec, kv_spec],
        out_shape=[jax.ShapeDtypeStruct((n_batch, MB_HEADS, MB_HEAD_DIM, t), _F32), kv_shape, kv_shape],
        input_output_aliases=aliases,
        compiler_params=_cparams(("parallel", "parallel")),
        name="mb_prep_t",
    )(z, z, z, cos_t, sin_t, jnp.tile(qn, MB_HEADS).reshape(1, MB_WIDTH),
      jnp.tile(kn, MB_HEADS).reshape(1, MB_WIDTH), seg, *extra)


def _top3_select(gate, axis):
    n = gate.shape[axis]
    idx = lax.broadcasted_iota(jnp.int32, gate.shape, axis).astype(_F32)
    sel = jnp.zeros(gate.shape, _F32)
    picks = []
    g = gate
    for _ in range(MB_TOPK):
        m = jnp.max(g, axis=axis, keepdims=True)
        first = jnp.min(jnp.where(g == m, idx, float(n)), axis=axis, keepdims=True)
        pick = idx == first
        sel = jnp.where(pick, 1.0, sel)
        picks.append(first)
        g = jnp.where(pick, BELOW_NEG, g)
    return sel, picks


AUG_DEPTH = 2 * MB_HEAD_DIM
V_AUG = LANES
LOG2_E = 1.4426950408889634


def _mb_attn_kernel(qt_ref, kt_ref, vt_ref, o_ref, kt_aug_scr, v_aug_scr, km_scr, s_scr, *, n_blk):
    i = pl.program_id(2)
    heads = range(ATTN_HEADS)
    filler = n_blk
    ext_rows = AUG_DEPTH - MB_HEAD_DIM

    @pl.when(i == 0)
    def _():
        lane = lax.broadcasted_iota(jnp.int32, (MB_HEAD_DIM, LANES), 1)
        ext_row = lax.broadcasted_iota(jnp.int32, (ext_rows, MB_BLOCK), 0)
        ones_row = jnp.where(lax.broadcasted_iota(jnp.int32, (V_AUG - MB_HEAD_DIM, MB_BLOCK), 0) == 0, 1.0, 0.0)
        for h in heads:
            km_t = jnp.zeros((MB_HEAD_DIM, LANES), _F32)
            for j in range(n_blk):
                kt_blk = kt_ref[0, 0, h, :, j * MB_BLOCK:(j + 1) * MB_BLOCK]
                vt_blk = vt_ref[0, 0, h, :, j * MB_BLOCK:(j + 1) * MB_BLOCK]
                kt_aug_scr[h, j] = jnp.concatenate([kt_blk, jnp.where(ext_row == j, 1.0, 0.0)], axis=0).astype(_BF)
                v_aug_scr[h, j] = jnp.concatenate([vt_blk, ones_row], axis=0).T.astype(_BF)
                mean = jnp.sum(kt_blk, axis=1, keepdims=True) * (1.0 / MB_BLOCK)
                km_t = jnp.where(lane == j, mean, km_t)
            km_scr[h] = km_t.T[:n_blk]
            kt_aug_scr[h, filler] = jnp.concatenate(
                [jnp.zeros((MB_HEAD_DIM, MB_BLOCK), _F32), jnp.where(ext_row == filler, 1.0, 0.0)],
                axis=0).astype(_BF)
            v_aug_scr[h, filler] = jnp.zeros((MB_BLOCK, V_AUG), _BF)

    blk_row = lax.broadcasted_iota(jnp.int32, (n_blk, MB_BLOCK), 0)
    row8 = lax.broadcasted_iota(jnp.int32, (SUBLANES, MB_BLOCK), 0)
    q_aug = []
    for h in heads:
        q_t = qt_ref[0, h]
        gate = _dot(km_scr[h], q_t, precision=_HIGHEST)
        eligible = blk_row < i
        sel, _ = _top3_select(jnp.where(eligible, gate, NEG), 0)
        sel = jnp.where(jnp.logical_or(blk_row == i, eligible), jnp.where(blk_row == i, 1.0, sel), 0.0)
        bias = (sel - 1.0) * (-NEG)
        q_s = q_t * (MB_HEAD_DIM ** -0.5 * LOG2_E)
        tail = jnp.where(row8 == 0, NEG, 0.0)
        pad = jnp.zeros((ext_rows - n_blk - SUBLANES, MB_BLOCK), _F32)
        q_aug_t = jnp.concatenate([q_s, bias, tail, pad], axis=0)
        q_aug.append(q_aug_t.T.astype(_BF))

    n_pairs = (i + 1) // 2

    def pair_blocks(jj):
        j0 = 2 * jj
        return (j0, jnp.where(j0 + 1 < i, j0 + 1, filler))

    def fold_max(s):
        return jnp.maximum(s[:, :LANES], s[:, LANES:])

    query_ge_key = (lax.broadcasted_iota(jnp.int32, (MB_BLOCK, MB_BLOCK), 0)
                    >= lax.broadcasted_iota(jnp.int32, (MB_BLOCK, MB_BLOCK), 1))
    m_init = []
    for h in heads:
        s = jnp.where(query_ge_key, _dot(q_aug[h], kt_aug_scr[h, i]), NEG)
        s_scr[h, i] = s
        m_init.append(fold_max(s))

    def score_body(jj, m):
        m = list(m)
        for j in pair_blocks(jj):
            for h in heads:
                s = _dot(q_aug[h], kt_aug_scr[h, j])
                s_scr[h, j] = s
                m[h] = jnp.maximum(m[h], fold_max(s))
        return tuple(m)

    m = lax.fori_loop(0, n_pairs, score_body, tuple(m_init))
    m_b = [jnp.broadcast_to(jnp.max(m[h], axis=1, keepdims=True), (MB_BLOCK, LANES)) for h in heads]

    def weighted_values(h, j):
        s = s_scr[h, j]
        p = jnp.concatenate([jnp.exp2(s[:, :LANES] - m_b[h]), jnp.exp2(s[:, LANES:] - m_b[h])], axis=1)
        return _dot(p.astype(_BF), v_aug_scr[h, j])

    def acc_body(jj, acc):
        acc = list(acc)
        for j in pair_blocks(jj):
            for h in heads:
                acc[h] = acc[h] + weighted_values(h, j)
        return tuple(acc)

    acc = lax.fori_loop(0, n_pairs, acc_body, tuple(weighted_values(h, i) for h in heads))
    o_ref[...] = jnp.concatenate(
        [acc[h][:, :MB_HEAD_DIM] / acc[h][:, MB_HEAD_DIM:MB_HEAD_DIM + 1] for h in heads], axis=1)


def _mb_attn(q_t, k_t_all, v_t_all, layer, n_batch, t):
    n_blk = t // MB_BLOCK
    assert n_blk % SUBLANES == 0 and MB_HEAD_DIM + n_blk + SUBLANES <= AUG_DEPTH
    n_hp = MB_HEADS // ATTN_HEADS
    kern = functools.partial(_mb_attn_kernel, n_blk=n_blk)
    kv_spec = pl.BlockSpec((1, 1, ATTN_HEADS, MB_HEAD_DIM, t), lambda b, hp, i: (b, layer, hp, 0, 0))
    return pl.pallas_call(
        kern,
        grid=(n_batch, n_hp, n_blk),
        in_specs=[pl.BlockSpec((1, ATTN_HEADS, MB_HEAD_DIM, MB_BLOCK), lambda b, hp, i: (b, hp, 0, i)),
                  kv_spec, kv_spec],
        out_specs=pl.BlockSpec((MB_BLOCK, ATTN_HEADS * MB_HEAD_DIM), lambda b, hp, i: (b * n_blk + i, hp)),
        out_shape=jax.ShapeDtypeStruct((n_batch * t, MB_WIDTH), _F32),
        scratch_shapes=[pltpu.VMEM((ATTN_HEADS, n_blk + 1, AUG_DEPTH, MB_BLOCK), _BF),
                        pltpu.VMEM((ATTN_HEADS, n_blk + 1, MB_BLOCK, V_AUG), _BF),
                        pltpu.VMEM((ATTN_HEADS, n_blk, MB_HEAD_DIM), _F32),
                        pltpu.VMEM((ATTN_HEADS, n_blk + 1, MB_BLOCK, MB_BLOCK), _F32)],
        compiler_params=_cparams(("parallel", "parallel", "arbitrary")),
        name="mb_attn",
    )(q_t, k_t_all, v_t_all)


def _kmean_kernel(pt_ref, *refs):
    page_refs, out_ref = refs[:KMEAN_PAGES_PER_STEP], refs[KMEAN_PAGES_PER_STEP]
    for jb in range(KMEAN_PAGES_PER_STEP // PAGES_PER_BLOCK):
        tot = None
        for p in range(PAGES_PER_BLOCK):
            page = page_refs[jb * PAGES_PER_BLOCK + p][0, 0]
            tot = page if tot is None else tot + page
        out_ref[0, 0, 0, :, :, jb:jb + 1] = jnp.sum(tot, axis=-1, keepdims=True) * (1.0 / MB_BLOCK)


def _kmean(cache_kt, pt_flat, n_layers, n_batch, n_pages):
    n_steps = n_pages // KMEAN_PAGES_PER_STEP
    blocks_per_step = KMEAN_PAGES_PER_STEP // PAGES_PER_BLOCK

    def page_spec(p):
        return pl.BlockSpec((1, 1, MB_HEADS, MB_HEAD_DIM, PAGE_SIZE),
                            lambda l, b, c, pt, p=p: (pt[b * n_pages + c * KMEAN_PAGES_PER_STEP + p], l, 0, 0, 0))

    grid_spec = pltpu.PrefetchScalarGridSpec(
        num_scalar_prefetch=1,
        grid=(n_layers, n_batch, n_steps),
        in_specs=[page_spec(p) for p in range(KMEAN_PAGES_PER_STEP)],
        out_specs=pl.BlockSpec((1, 1, 1, MB_HEADS, MB_HEAD_DIM, blocks_per_step),
                               lambda l, b, c, pt: (l, b, c, 0, 0, 0)),
    )
    return pl.pallas_call(
        _kmean_kernel,
        grid_spec=grid_spec,
        out_shape=jax.ShapeDtypeStruct((n_layers, n_batch, n_steps, MB_HEADS, MB_HEAD_DIM, blocks_per_step), _F32),
        compiler_params=_cparams(("parallel", "parallel", "parallel")),
        name="kmean",
    )(pt_flat, *([cache_kt] * KMEAN_PAGES_PER_STEP))


GATE_PAGE_LANE0 = SUBLANES


def _gate_kernel(q_ref, km_ref, pt_ref, idx_ref, *, own):
    n_past = km_ref.shape[3]
    n_pages = pt_ref.shape[2]
    col = lax.broadcasted_iota(jnp.int32, (SAMPLE_T_PAD, n_past), 1)
    lane = lax.broadcasted_iota(jnp.int32, (SAMPLE_T_PAD, LANES), 1)
    page_col = lax.broadcasted_iota(jnp.int32, (SAMPLE_T_PAD, n_pages), 1).astype(_F32)
    pt_row = pt_ref[0].astype(_F32)
    for h in range(MB_HEADS):
        gate = _dot(q_ref[0, h], km_ref[0, h], precision=_HIGHEST)
        _, picks = _top3_select(jnp.where(col < own, gate, NEG), 1)
        out = jnp.zeros((SAMPLE_T_PAD, LANES), jnp.int32)
        for r, first in enumerate(picks):
            out = jnp.where(lane == r, first.astype(jnp.int32), out)
            for p in range(PAGES_PER_BLOCK):
                logical = first * PAGES_PER_BLOCK + p
                phys = jnp.sum(jnp.where(page_col == logical, pt_row, 0.0), axis=1, keepdims=True)
                out = jnp.where(lane == GATE_PAGE_LANE0 + r * PAGES_PER_BLOCK + p, phys.astype(jnp.int32), out)
        idx_ref[0, h] = out


def _gate(q_s, kmean_l, page_table, own):
    n_batch = q_s.shape[0]
    n_past = kmean_l.shape[3]
    n_pages = page_table.shape[1]
    return pl.pallas_call(
        functools.partial(_gate_kernel, own=own),
        grid=(n_batch,),
        in_specs=[pl.BlockSpec((1, MB_HEADS, SAMPLE_T_PAD, MB_HEAD_DIM), lambda b: (b, 0, 0, 0)),
                  pl.BlockSpec((1, MB_HEADS, MB_HEAD_DIM, n_past), lambda b: (b, 0, 0, 0)),
                  pl.BlockSpec((1, 1, n_pages), lambda b: (b, 0, 0))],
        out_specs=pl.BlockSpec((1, MB_HEADS, SAMPLE_T_PAD, LANES), lambda b: (b, 0, 0, 0)),
        out_shape=jax.ShapeDtypeStruct((n_batch, MB_HEADS, SAMPLE_T_PAD, LANES), jnp.int32),
        compiler_params=_cparams(("parallel",)),
        name="mb_gate",
    )(q_s, kmean_l, page_table.reshape(n_batch, 1, n_pages))


def _sample_attn_kernel(pg_ref, idx_ref, q_ref, kn_ref, vn_ref, ck_ref, cv_ref, o_ref, kbuf, vbuf, sem,
                        *, t_s, own, layer):
    n_slots = t_s * MB_TOPK * PAGES_PER_BLOCK
    b = pl.program_id(0)
    row = lax.broadcasted_iota(jnp.int32, (SAMPLE_T_PAD, 1), 0)

    def page_copies(h, par):
        base = (b * MB_HEADS + h) * n_slots
        out = []
        for n in range(n_slots):
            page = pg_ref[base + n]
            out.append(pltpu.make_async_copy(ck_ref.at[page, layer, h], kbuf.at[par, n], sem.at[0, par]))
            out.append(pltpu.make_async_copy(cv_ref.at[page, layer, h], vbuf.at[par, n], sem.at[1, par]))
        return out

    for cp in page_copies(0, 0):
        cp.start()

    def head_body(h, carry):
        par = h % 2

        @pl.when(h + 1 < MB_HEADS)
        def _():
            for cp in page_copies(h + 1, 1 - par):
                cp.start()

        for cp in page_copies(h, par):
            cp.wait()

        q = q_ref[0, h] * (MB_HEAD_DIM ** -0.5)
        q_bf = q.astype(_BF)
        scores = []
        for t in range(t_s):
            for j in range(MB_TOPK):
                blk = idx_ref[((b * MB_HEADS + h) * t_s + t) * MB_TOPK + j]
                allowed = row == jnp.where(blk < own, t, -1)
                for p in range(PAGES_PER_BLOCK):
                    slot = (t * MB_TOPK + j) * PAGES_PER_BLOCK + p
                    s = _dot(q_bf, kbuf[par, slot].astype(_BF))
                    scores.append(jnp.where(allowed, s, NEG))
        kn = kn_ref[0, h]
        vn = vn_ref[0, h]
        own_scores = []
        for c in range(t_s):
            s = jnp.sum(q * kn[c:c + 1, :], axis=1, keepdims=True)
            own_scores.append(jnp.where(row >= c, s, NEG))

        m = own_scores[0]
        for s in own_scores[1:]:
            m = jnp.maximum(m, s)
        for s in scores:
            m = jnp.maximum(m, jnp.max(s, axis=1, keepdims=True))
        l = jnp.zeros((SAMPLE_T_PAD, 1), _F32)
        acc = jnp.zeros((SAMPLE_T_PAD, MB_HEAD_DIM), _F32)
        for c, s in enumerate(own_scores):
            p = jnp.exp(s - m)
            l = l + p
            acc = acc + p * vn[c:c + 1, :]
        for slot, s in enumerate(scores):
            p = jnp.exp(s - m)
            l = l + jnp.sum(p, axis=1, keepdims=True)
            acc = acc + _dot_nt(p.astype(_BF), vbuf[par, slot].astype(_BF))
        o_ref[0, h] = acc / l
        return carry

    lax.fori_loop(0, MB_HEADS, head_body, 0)


def _sample_attn(q_s, k_s, v_s, cache_kt, cache_vt, pages_flat, idx_flat, layer, t_s, own):
    n_batch = q_s.shape[0]
    n_slots = t_s * MB_TOPK * PAGES_PER_BLOCK
    new_spec = pl.BlockSpec((1, MB_HEADS, SAMPLE_T_PAD, MB_HEAD_DIM), lambda b, pg, idx: (b, 0, 0, 0))
    hbm = pl.BlockSpec(memory_space=pl.ANY)
    grid_spec = pltpu.PrefetchScalarGridSpec(
        num_scalar_prefetch=2,
        grid=(n_batch,),
        in_specs=[new_spec, new_spec, new_spec, hbm, hbm],
        out_specs=new_spec,
        scratch_shapes=[pltpu.VMEM((2, n_slots, MB_HEAD_DIM, PAGE_SIZE), _F32),
                        pltpu.VMEM((2, n_slots, MB_HEAD_DIM, PAGE_SIZE), _F32),
                        pltpu.SemaphoreType.DMA((2, 2))],
    )
    return pl.pallas_call(
        functools.partial(_sample_attn_kernel, t_s=t_s, own=own, layer=layer),
        grid_spec=grid_spec,
        out_shape=jax.ShapeDtypeStruct((n_batch, MB_HEADS, SAMPLE_T_PAD, MB_HEAD_DIM), _F32),
        compiler_params=_cparams(("parallel",)),
        name="mb_sample_attn",
    )(pages_flat, idx_flat, q_s, k_s, v_s, cache_kt, cache_vt)


def _pool_kernel(u_ref, hist_ref, w_ref, sc_ref, y_ref, hist_out_ref, ext_scr, *, tm, pos0, t_last):
    ti = pl.program_id(1)

    @pl.when(ti == 0)
    def _():
        ext_scr[0:POOL_HALO, :] = hist_ref[0]

    u = u_ref[...].astype(_F32)
    ext_scr[POOL_HALO:POOL_HALO + tm, :] = u
    pos = pos0 + ti * tm + lax.broadcasted_iota(jnp.int32, (tm, 1), 0)
    for g, w in enumerate(POOL_WINDOWS):
        sl = slice(g * POOL_GROUP_DIM, (g + 1) * POOL_GROUP_DIM)
        tot = u[:, sl]
        for d in range(1, w):
            tot = tot + ext_scr[POOL_HALO - d:POOL_HALO - d + tm, sl]
        cnt = jnp.minimum(pos + 1, w).astype(_F32)
        diff = tot / cnt - u[:, sl]
        y = _dot(diff.astype(_BF), w_ref[g])
        y_ref[:, sl] = y * sc_ref[:, sl]

    @pl.when(ti == pl.num_programs(1) - 1)
    def _():
        hist_out_ref[0] = ext_scr[t_last:t_last + POOL_HALO, :]

    @pl.when(ti < pl.num_programs(1) - 1)
    def _():
        ext_scr[0:POOL_HALO, :] = ext_scr[tm:tm + POOL_HALO, :]


def _pool(z, hist16, pool_w_bf, pool_scale, n_batch, t_pad, tm, pos0, t_valid):
    n_t = t_pad // tm
    t_last = t_valid - (n_t - 1) * tm
    hist_spec = pl.BlockSpec((1, POOL_HALO, POOL_WIDTH), lambda b, ti: (b, 0, 0))
    kern = functools.partial(_pool_kernel, tm=tm, pos0=pos0, t_last=t_last)
    return pl.pallas_call(
        kern,
        grid=(n_batch, n_t),
        in_specs=[pl.BlockSpec((tm, POOL_WIDTH), lambda b, ti: (b * n_t + ti, COL_UC)),
                  hist_spec,
                  pl.BlockSpec((POOL_GROUPS, POOL_GROUP_DIM, POOL_GROUP_DIM), lambda b, ti: (0, 0, 0)),
                  pl.BlockSpec((1, POOL_WIDTH), lambda b, ti: (0, 0))],
        out_specs=[pl.BlockSpec((tm, POOL_WIDTH), lambda b, ti: (b * n_t + ti, 0)), hist_spec],
        out_shape=[jax.ShapeDtypeStruct((n_batch * t_pad, POOL_WIDTH), _F32),
                   jax.ShapeDtypeStruct((n_batch, POOL_HALO, POOL_WIDTH), _F32)],
        scratch_shapes=[pltpu.VMEM((POOL_HALO + tm, POOL_WIDTH), _F32)],
        compiler_params=_cparams(("parallel", "arbitrary")),
        name="pool",
    )(z, hist16, pool_w_bf, pool_scale.reshape(1, POOL_WIDTH))


def _merge_kernel(x_ref, a_ref, b_ref, c_ref, g0_ref, g1_ref, g2_ref, wb_ref, wo_ref, o_ref):
    merged = None
    for br_ref, g_ref, n in ((a_ref, g0_ref, 0), (b_ref, g1_ref, 1), (c_ref, g2_ref, 2)):
        proj = _dot(br_ref[...].astype(_BF), wb_ref[n])
        term = _sigmoid(g_ref[...].astype(_F32)) * proj
        merged = term if merged is None else merged + term
    o_ref[...] = x_ref[...] + _dot(merged.astype(_BF), wo_ref[...])


def _merge(x, a, b, c, z, wb_bf, wo_bf, tm):
    n = x.shape[0]
    row = lambda w: pl.BlockSpec((tm, w), lambda i: (i, 0))
    gate = lambda k: pl.BlockSpec((tm, D_MODEL), lambda i, k=k: (i, COL_GATE_1024 + k))
    return pl.pallas_call(
        _merge_kernel,
        grid=(n // tm,),
        in_specs=[row(D_MODEL), row(BRANCH_WIDTH), row(BRANCH_WIDTH), row(BRANCH_WIDTH),
                  gate(0), gate(1), gate(2),
                  pl.BlockSpec((N_BRANCH, BRANCH_WIDTH, D_MODEL), lambda i: (0, 0, 0)),
                  pl.BlockSpec((D_MODEL, D_MODEL), lambda i: (0, 0))],
        out_specs=row(D_MODEL),
        out_shape=jax.ShapeDtypeStruct((n, D_MODEL), _F32),
        compiler_params=_cparams(("parallel",)),
        name="merge",
    )(x, a, b, c, z, z, z, wb_bf, wo_bf)


def _ffn_kernel(x_ref, g_ref, w1_ref, w2_ref, o_ref):
    x = x_ref[...]
    ms = jnp.mean(x * x, axis=-1, keepdims=True)
    h = (x * lax.rsqrt(ms + EPS) * g_ref[...]).astype(_BF)
    hid = jnp.maximum(_dot(h, w1_ref[...]), 0.0)
    hid = (hid * hid).astype(_BF)
    o_ref[...] = x + _dot(hid, w2_ref[...])


def _ffn(x, g, w1_bf, w2_bf, tm):
    n = x.shape[0]
    row = pl.BlockSpec((tm, D_MODEL), lambda i: (i, 0))
    return pl.pallas_call(
        _ffn_kernel,
        grid=(n // tm,),
        in_specs=[row, pl.BlockSpec((1, D_MODEL), lambda i: (0, 0)),
                  pl.BlockSpec((D_MODEL, D_FF), lambda i: (0, 0)),
                  pl.BlockSpec((D_FF, D_MODEL), lambda i: (0, 0))],
        out_specs=row,
        out_shape=jax.ShapeDtypeStruct((n, D_MODEL), _F32),
        compiler_params=_cparams(("parallel",)),
        name="ffn",
    )(x, g.reshape(1, D_MODEL), w1_bf, w2_bf)


def _rope_tables(pos):
    half = MB_HEAD_DIM // 2
    inv = ROPE_THETA ** (-jnp.arange(half, dtype=_F32) / half)
    ang = pos.astype(_F32)[:, None] * inv[None, :]
    cos, sin = jnp.cos(ang), jnp.sin(ang)
    cos_t = jnp.concatenate([cos, cos, cos, cos], axis=1)
    sin_t = jnp.concatenate([-sin, sin, -sin, sin], axis=1)
    return cos_t, sin_t


def _tile_rows(n, cap):
    t = cap
    while n % t:
        t //= 2
    return t


def _layer(x, *, n_batch, t_pad, t_valid, pos0, chunk, sub, lb, s0_t, hist16, moba, w):
    n = x.shape[0]
    z_dtype = _BF if t_pad % (2 * SUBLANES) == 0 else _F32
    z = _inproj(x, w["norm_mix"], w["w_in"], _tile_rows(n, 1024), D_IN // 4, z_dtype)
    a_out, s_new_t = _hgrn(z, lb, w["hg_onorm"], s0_t, n_batch, t_pad, chunk, sub, t_valid)
    b_out, kv = moba(z)
    c_out, hist_new = _pool(z, hist16, w["pool_w"], w["pool_scale"], n_batch, t_pad, _tile_rows(t_pad, 512),
                            pos0, t_valid)
    x1 = _merge(x, a_out, b_out, c_out, z, w["w_branch"], w["w_out"], _tile_rows(n, 512))
    x2 = _ffn(x1, w["norm_ffn"], w["w_ff1"], w["w_ff2"], _tile_rows(n, 256))
    return x2, kv, s_new_t, hist_new


def kernel(x_prompt, x_sample, cache_k, cache_v, state_hgrn, state_pool, page_table, norm_mix, w_in, hg_lb,
           hg_onorm, mb_qnorm, mb_knorm, pool_w, pool_scale, w_branch, w_out, norm_ffn, w_ff1, w_ff2):
    n_b, t_p, _ = x_prompt.shape
    n_db, t_s, _ = x_sample.shape
    depth = w_in.shape[0]
    n_pages = page_table.shape[1]
    past_len = n_pages * PAGE_SIZE
    assert t_p % MB_BLOCK == 0 and t_p % HG_CHUNK == 0
    assert past_len % MB_BLOCK == 0 and t_s <= SAMPLE_T_PAD and n_pages % KMEAN_PAGES_PER_STEP == 0
    own_s = past_len // MB_BLOCK

    lb_soft = jax.nn.softmax(hg_lb.astype(_F32), axis=0)
    lb_all = jnp.cumsum(lb_soft, axis=0) - lb_soft[0:1]

    seg = (jnp.arange(MB_WIDTH)[:, None] // MB_HEAD_DIM == jnp.arange(MB_WIDTH)[None, :] // MB_HEAD_DIM).astype(_BF)
    cos_p, sin_p = _rope_tables(jnp.arange(t_p, dtype=jnp.int32))
    cos_s, sin_s = _rope_tables(past_len + jnp.arange(SAMPLE_T_PAD, dtype=jnp.int32))

    cache_kt = jnp.swapaxes(cache_k, 3, 4)
    cache_vt = jnp.swapaxes(cache_v, 3, 4)
    pt_flat = page_table.reshape(-1).astype(jnp.int32)
    kmean_all = _kmean(cache_kt, pt_flat, depth, n_db, n_pages)
    kmean_all = jnp.transpose(kmean_all, (0, 1, 3, 4, 2, 5)).reshape(depth, n_db, MB_HEADS, MB_HEAD_DIM, own_s)

    xp = x_prompt.reshape(n_b * t_p, D_MODEL)
    xs = jnp.pad(x_sample, ((0, 0), (0, SAMPLE_T_PAD - t_s), (0, 0))).reshape(n_db * SAMPLE_T_PAD, D_MODEL)

    s0_p = jnp.zeros((n_b, HG_HEADS, HG_DV, HG_DK), _F32)
    hist_p = jnp.zeros((n_b, POOL_HALO, POOL_WIDTH), _F32)

    kv_p = None
    ks_l, vs_l, sp_l, ss_l, hp_l, hs_l = [], [], [], [], [], []
    for l in range(depth):
        w = dict(norm_mix=norm_mix[l], w_in=w_in[l].astype(_BF), hg_onorm=hg_onorm[l], mb_qnorm=mb_qnorm[l],
                 mb_knorm=mb_knorm[l], pool_w=pool_w[l].astype(_BF), pool_scale=pool_scale[l],
                 w_branch=w_branch[l].astype(_BF), w_out=w_out[l].astype(_BF), norm_ffn=norm_ffn[l],
                 w_ff1=w_ff1[l].astype(_BF), w_ff2=w_ff2[l].astype(_BF))

        def moba_p(z, l=l, w=w, kv_prev=kv_p):
            q_t, k_t_all, v_t_all = _mb_prep_t(z, cos_p, sin_p, w["mb_qnorm"], w["mb_knorm"], seg, n_b, t_p,
                                               _tile_rows(t_p, 512), l, depth, kv_prev)
            return _mb_attn(q_t, k_t_all, v_t_all, l, n_b, t_p), (k_t_all, v_t_all)

        xp, kv_p, sp_t, hp = _layer(
            xp, n_batch=n_b, t_pad=t_p, t_valid=t_p, pos0=0, chunk=HG_CHUNK, sub=HG_SUB, lb=lb_all[l],
            s0_t=s0_p, hist16=hist_p, moba=moba_p, w=w)

        def moba_s(z, l=l, w=w):
            q, k, v = _mb_prep(z, cos_s, sin_s, w["mb_qnorm"], w["mb_knorm"], seg, n_db, SAMPLE_T_PAD, SAMPLE_T_PAD)
            idx = _gate(q, kmean_all[l], page_table, own_s)
            idx_flat = idx[:, :, :t_s, :MB_TOPK].reshape(-1)
            pages_flat = idx[:, :, :t_s, GATE_PAGE_LANE0:GATE_PAGE_LANE0 + MB_TOPK * PAGES_PER_BLOCK].reshape(-1)
            o = _sample_attn(q, k, v, cache_kt, cache_vt, pages_flat, idx_flat, l, t_s, own_s)
            return jnp.transpose(o, (0, 2, 1, 3)).reshape(n_db * SAMPLE_T_PAD, MB_WIDTH), (k, v)

        s0_s = jnp.swapaxes(state_hgrn[l], -1, -2)
        hist_s = jnp.pad(state_pool[l], ((0, 0), (POOL_HALO - POOL_HIST, 0), (0, 0)))
        xs, (ks, vs), ss_t, hs = _layer(
            xs, n_batch=n_db, t_pad=SAMPLE_T_PAD, t_valid=t_s, pos0=past_len, chunk=SAMPLE_T_PAD,
            sub=SAMPLE_T_PAD, lb=lb_all[l], s0_t=s0_s, hist16=hist_s, moba=moba_s, w=w)

        ks_l.append(ks[:, :, :t_s]); vs_l.append(vs[:, :, :t_s])
        sp_l.append(jnp.swapaxes(sp_t, -1, -2)); ss_l.append(jnp.swapaxes(ss_t, -1, -2))
        hp_l.append(hp[:, POOL_HALO - POOL_HIST:]); hs_l.append(hs[:, POOL_HALO - POOL_HIST:])

    y_p = xp.reshape(n_b, t_p, D_MODEL)
    y_s = xs.reshape(n_db, SAMPLE_T_PAD, D_MODEL)[:, :t_s]
    k_p = jnp.swapaxes(kv_p[0], 3, 4)
    v_p = jnp.swapaxes(kv_p[1], 3, 4)
    return (y_p, y_s, k_p, v_p, jnp.stack(ks_l, axis=1), jnp.stack(vs_l, axis=1),
            jnp.stack(sp_l, axis=0), jnp.stack(ss_l, axis=0), jnp.stack(hp_l, axis=0), jnp.stack(hs_l, axis=0))
```

```python
import functools

import jax
import jax.numpy as jnp
from jax import lax
from jax.experimental import pallas as pl
from jax.experimental.pallas import tpu as pltpu

D_MODEL = 1024
PAGE_SIZE = 128
HG_HEADS = 4
HG_DK = 128
HG_DV = 128
HG_WIDTH = HG_HEADS * HG_DV
HG_CHUNK = 64
HG_SUB = 16
HG_SEQS_PER_STEP = 2
MB_HEADS = 8
MB_HEAD_DIM = 64
MB_WIDTH = MB_HEADS * MB_HEAD_DIM
MB_BLOCK = 256
MB_TOPK = 3
ROPE_THETA = 10000.0
POOL_WINDOWS = (2, 4, 8, 16)
POOL_GROUPS = 4
POOL_GROUP_DIM = 128
POOL_WIDTH = POOL_GROUPS * POOL_GROUP_DIM
POOL_HIST = 15
POOL_HALO = 16
N_BRANCH = 3
BRANCH_WIDTH = 512
D_FF = 4 * D_MODEL
EPS = 1e-6
NEG = -1e30
BELOW_NEG = -3e38
F_FLOOR = 1e-30
D_IN = 4 * HG_WIDTH + 3 * MB_WIDTH + POOL_WIDTH + N_BRANCH * D_MODEL
COL_QA, COL_FA, COL_IA, COL_GA, COL_QB, COL_KB, COL_VB, COL_UC = range(8)
COL_GATE_1024 = 4

LANES = 128
SUBLANES = 8
SAMPLE_T_PAD = SUBLANES
VMEM_LIMIT = 56 * 1024 * 1024
PAGES_PER_BLOCK = MB_BLOCK // PAGE_SIZE
KMEAN_PAGES_PER_STEP = 64
HEADS_PER_STEP = LANES // MB_HEAD_DIM
ATTN_HEADS = 4

_BF = jnp.bfloat16
_F32 = jnp.float32
_HIGHEST = lax.Precision.HIGHEST


def _cparams(sem, vmem=VMEM_LIMIT):
    return pltpu.CompilerParams(dimension_semantics=sem, vmem_limit_bytes=vmem)


def _dot(a, b, precision=None):
    return jnp.dot(a, b, preferred_element_type=_F32, precision=precision)


def _dot_nt(a, b, precision=None):
    return lax.dot_general(a, b, (((1,), (1,)), ((), ())), preferred_element_type=_F32, precision=precision)


def _dot_tn(a, b):
    return lax.dot_general(a, b, (((0,), (0,)), ((), ())), preferred_element_type=_F32)


def _sigmoid(x):
    return 1.0 / (1.0 + jnp.exp(-x))


def _inproj_kernel(x_ref, g_ref, w_ref, z_ref, h_ref):
    @pl.when(pl.program_id(1) == 0)
    def _():
        x = x_ref[...]
        ms = jnp.mean(x * x, axis=-1, keepdims=True)
        h_ref[...] = (x * lax.rsqrt(ms + EPS) * g_ref[...]).astype(_BF)

    z_ref[...] = _dot(h_ref[...], w_ref[...]).astype(z_ref.dtype)


def _inproj(x, g, w_bf, tm, tn, z_dtype):
    n = x.shape[0]
    return pl.pallas_call(
        _inproj_kernel,
        grid=(n // tm, D_IN // tn),
        in_specs=[
            pl.BlockSpec((tm, D_MODEL), lambda i, j: (i, 0)),
            pl.BlockSpec((1, D_MODEL), lambda i, j: (0, 0)),
            pl.BlockSpec((D_MODEL, tn), lambda i, j: (0, j)),
        ],
        out_specs=pl.BlockSpec((tm, tn), lambda i, j: (i, j)),
        out_shape=jax.ShapeDtypeStruct((n, D_IN), z_dtype),
        scratch_shapes=[pltpu.VMEM((tm, D_MODEL), _BF)],
        compiler_params=_cparams(("parallel", "arbitrary")),
        name="inproj",
    )(x, g.reshape(1, D_MODEL), w_bf)


def _hgrn_kernel(q_ref, f_ref, i_ref, g_ref, lb_ref, on_ref, s0_ref, a_ref, s_out_ref,
                 st_scr, b_scr, k_scr, v_scr, *, chunk, cps, sub, t_valid):
    ci = pl.program_id(1)

    @pl.when(ci == 0)
    def _():
        for cc in range(cps):
            for h in range(HG_HEADS):
                st_scr[cc * HG_HEADS + h] = s0_ref[cc, h]

    row = lax.broadcasted_iota(jnp.int32, (chunk, 1), 0)
    row8 = lax.broadcasted_iota(jnp.int32, (SUBLANES, 1), 0)
    tril = (lax.broadcasted_iota(jnp.int32, (chunk, chunk), 0)
            >= lax.broadcasted_iota(jnp.int32, (chunk, chunk), 1)).astype(_F32)
    n_sub = chunk // sub
    heads = range(HG_HEADS)
    lanes = [slice(h * HG_DK, (h + 1) * HG_DK) for h in heads]
    units = [(h, cc) for cc in range(cps) for h in heads]
    q_u, k_u, vbf_u, b_u = {}, {}, {}, {}

    for h, cc in units:
        sl, hc = lanes[h], h * cps + cc
        qr = q_ref[cc, :, sl].astype(_F32)
        fr = f_ref[cc, :, sl].astype(_F32)
        v = i_ref[cc, :, sl].astype(_F32)
        lb = lb_ref[:, sl]
        fval = lb + (1.0 - lb) * _sigmoid(fr)
        log2f = jnp.log2(jnp.maximum(fval, F_FLOOR))
        k = (1.0 - lb) * _sigmoid(-fr)
        if t_valid < chunk:
            valid = row < t_valid
            log2f = jnp.where(valid, log2f, 0.0)
            k = jnp.where(valid, k, 0.0)
        b = _dot(tril, log2f, precision=_HIGHEST)
        b_scr[hc] = b
        k_scr[hc] = k
        v_scr[hc] = v
        q_u[h, cc], k_u[h, cc], vbf_u[h, cc], b_u[h, cc] = qr * _sigmoid(qr), k, v.astype(_BF), b

    st = {(h, cc): st_scr[cc * HG_HEADS + h] for h, cc in units}
    acc = {}
    for cc in range(cps):
        for h in heads:
            q, k, v_bf, b = q_u[h, cc], k_u[h, cc], vbf_u[h, cc], b_u[h, cc]
            o = _dot_nt((q * jnp.exp2(b)).astype(_BF), st[h, cc].astype(_BF))
            b_last = b[chunk - 1:chunk]
            k_end = (k * jnp.exp2(b_last - b)).astype(_BF)
            st[h, cc] = st[h, cc] * jnp.exp2(b_last) + _dot_tn(v_bf, k_end)
            for si in range(n_sub):
                r0 = si * sub
                o_s = o[r0:r0 + sub]
                if si > 0:
                    ref = b[r0 - 1:r0]
                    q_dec = (q[r0:r0 + sub] * jnp.exp2(b[r0:r0 + sub] - ref)).astype(_BF)
                    k_dec = (k[:r0] * jnp.exp2(ref - b[:r0])).astype(_BF)
                    att = _dot_nt(q_dec, k_dec)
                    o_s = o_s + _dot(att.astype(_BF), v_bf[:r0])
                for g0 in range(0, sub, SUBLANES):
                    acc[h, cc, r0 + g0] = o_s[g0:g0 + SUBLANES]
    for h, cc in units:
        st_scr[cc * HG_HEADS + h] = st[h, cc]

    for si in range(n_sub):
        r0 = si * sub
        for g0 in range(0, sub, SUBLANES):
            rg = slice(r0 + g0, r0 + g0 + SUBLANES)
            for s in range(g0 + SUBLANES):
                rs = slice(r0 + s, r0 + s + 1)
                for h, cc in units:
                    hc = h * cps + cc
                    d = b_u[h, cc][rg] - b_scr[hc, rs, :]
                    if s > g0:
                        d = jnp.where(row8 >= s - g0, d, NEG)
                    p = q_u[h, cc][rg] * jnp.exp2(d) * k_scr[hc, rs, :]
                    acc[h, cc, r0 + g0] = acc[h, cc, r0 + g0] + jnp.sum(p, axis=-1, keepdims=True) * v_scr[hc, rs, :]

    for h, cc in units:
        sl = lanes[h]
        o = jnp.concatenate([acc[h, cc, r] for r in range(0, chunk, SUBLANES)], axis=0) if chunk > SUBLANES \
            else acc[h, cc, 0]
        gr = g_ref[cc, :, sl].astype(_F32)
        ms = jnp.mean(o * o, axis=-1, keepdims=True)
        a_ref[cc, :, sl] = o * lax.rsqrt(ms + EPS) * on_ref[...] * (gr * _sigmoid(gr))

    @pl.when(ci == pl.num_programs(1) - 1)
    def _():
        for cc in range(cps):
            for h in range(HG_HEADS):
                s_out_ref[cc, h] = st_scr[cc * HG_HEADS + h]


def _hgrn(z, lb, onorm, s0_t, n_batch, t_pad, chunk, sub, t_valid):
    cps = HG_SEQS_PER_STEP
    assert n_batch % cps == 0 and t_pad % chunk == 0
    n_chunks = t_pad // chunk
    z3 = z.reshape(n_batch, t_pad, D_IN)
    col = lambda c: pl.BlockSpec((cps, chunk, HG_WIDTH), lambda b, ci, c=c: (b, ci, c))
    state_spec = pl.BlockSpec((cps, HG_HEADS, HG_DV, HG_DK), lambda b, ci: (b, 0, 0, 0))
    kern = functools.partial(_hgrn_kernel, chunk=chunk, cps=cps, sub=sub, t_valid=t_valid)
    a_out, s_new = pl.pallas_call(
        kern,
        grid=(n_batch // cps, n_chunks),
        in_specs=[col(COL_QA), col(COL_FA), col(COL_IA), col(COL_GA),
                  pl.BlockSpec((1, HG_WIDTH), lambda b, ci: (0, 0)),
                  pl.BlockSpec((1, HG_DV), lambda b, ci: (0, 0)),
                  state_spec],
        out_specs=[pl.BlockSpec((cps, chunk, HG_WIDTH), lambda b, ci: (b, ci, 0)), state_spec],
        out_shape=[jax.ShapeDtypeStruct((n_batch, t_pad, HG_WIDTH), _F32),
                   jax.ShapeDtypeStruct((n_batch, HG_HEADS, HG_DV, HG_DK), _F32)],
        scratch_shapes=[pltpu.VMEM((HG_HEADS * cps, HG_DV, HG_DK), _F32),
                        pltpu.VMEM((HG_HEADS * cps, chunk, HG_DK), _F32),
                        pltpu.VMEM((HG_HEADS * cps, chunk, HG_DK), _F32),
                        pltpu.VMEM((HG_HEADS * cps, chunk, HG_DV), _F32)],
        compiler_params=_cparams(("parallel", "arbitrary")),
        name="hgrn",
    )(z3, z3, z3, z3, lb.reshape(1, HG_WIDTH), onorm.reshape(1, HG_DV), s0_t)
    return a_out.reshape(n_batch * t_pad, HG_WIDTH), s_new


def _norm_rope(x, w, cos, sin, seg):
    lane = lax.broadcasted_iota(jnp.int32, (1, LANES), 1)
    first_half = (lane % MB_HEAD_DIM) < (MB_HEAD_DIM // 2)
    sq = x * x
    hi = sq.astype(_BF)
    lo = (sq - hi.astype(_F32)).astype(_BF)
    ss = _dot(hi, seg) + _dot(lo, seg)
    xn = x * lax.rsqrt(ss * (1.0 / MB_HEAD_DIM) + EPS) * w
    outs = []
    for g in range(MB_WIDTH // LANES):
        xg = xn[:, g * LANES:(g + 1) * LANES]
        rot = jnp.where(first_half, pltpu.roll(xg, LANES - MB_HEAD_DIM // 2, 1),
                        pltpu.roll(xg, MB_HEAD_DIM // 2, 1))
        outs.append(xg * cos + rot * sin)
    return outs


def _mb_prep_kernel(q_ref, k_ref, v_ref, cos_ref, sin_ref, qn_ref, kn_ref, seg_ref,
                    qo_ref, ko_ref, vo_ref):
    q_g = _norm_rope(q_ref[...].astype(_F32), qn_ref[...], cos_ref[...], sin_ref[...], seg_ref[...])
    k_g = _norm_rope(k_ref[...].astype(_F32), kn_ref[...], cos_ref[...], sin_ref[...], seg_ref[...])
    v = v_ref[...].astype(_F32)
    for h in range(MB_HEADS):
        g, off = divmod(h * MB_HEAD_DIM, LANES)
        qo_ref[0, h] = q_g[g][:, off:off + MB_HEAD_DIM]
        ko_ref[0, h] = k_g[g][:, off:off + MB_HEAD_DIM]
        vo_ref[0, h] = v[:, h * MB_HEAD_DIM:(h + 1) * MB_HEAD_DIM]


def _mb_prep_t_kernel(q_ref, k_ref, v_ref, cos_ref, sin_ref, qn_ref, kn_ref, seg_ref, *rest):
    qo_ref, ko_ref, vo_ref = rest[-3:]
    q_g = _norm_rope(q_ref[...].astype(_F32), qn_ref[...], cos_ref[...], sin_ref[...], seg_ref[...])
    k_g = _norm_rope(k_ref[...].astype(_F32), kn_ref[...], cos_ref[...], sin_ref[...], seg_ref[...])
    v = v_ref[...].astype(_F32)
    for g in range(MB_WIDTH // LANES):
        q_t = q_g[g].T
        k_t = k_g[g].T
        v_t = v[:, g * LANES:(g + 1) * LANES].T
        for hh in range(HEADS_PER_STEP):
            h = g * HEADS_PER_STEP + hh
            rows = slice(hh * MB_HEAD_DIM, (hh + 1) * MB_HEAD_DIM)
            qo_ref[0, h] = q_t[rows]
            ko_ref[0, 0, h] = k_t[rows]
            vo_ref[0, 0, h] = v_t[rows]


def _prep_common_specs(n_t, tm):
    col = lambda c: pl.BlockSpec((tm, MB_WIDTH), lambda b, ti, c=c: (b * n_t + ti, c))
    tab = pl.BlockSpec((tm, LANES), lambda b, ti: (ti, 0))
    vec = pl.BlockSpec((1, MB_WIDTH), lambda b, ti: (0, 0))
    return [col(COL_QB), col(COL_KB), col(COL_VB), tab, tab, vec, vec,
            pl.BlockSpec((MB_WIDTH, MB_WIDTH), lambda b, ti: (0, 0))]


def _mb_prep(z, cos_t, sin_t, qn, kn, seg, n_batch, t_pad, tm):
    n_t = t_pad // tm
    out = pl.BlockSpec((1, MB_HEADS, tm, MB_HEAD_DIM), lambda b, ti: (b, 0, ti, 0))
    shp = jax.ShapeDtypeStruct((n_batch, MB_HEADS, t_pad, MB_HEAD_DIM), _F32)
    return pl.pallas_call(
        _mb_prep_kernel,
        grid=(n_batch, n_t),
        in_specs=_prep_common_specs(n_t, tm),
        out_specs=[out, out, out],
        out_shape=[shp, shp, shp],
        compiler_params=_cparams(("parallel", "parallel")),
        name="mb_prep",
    )(z, z, z, cos_t, sin_t, jnp.tile(qn, MB_HEADS).reshape(1, MB_WIDTH),
      jnp.tile(kn, MB_HEADS).reshape(1, MB_WIDTH), seg)


def _mb_prep_t(z, cos_t, sin_t, qn, kn, seg, n_batch, t, tm, layer, depth, kv_prev):
    n_t = t // tm
    q_spec = pl.BlockSpec((1, MB_HEADS, MB_HEAD_DIM, tm), lambda b, ti: (b, 0, 0, ti))
    kv_spec = pl.BlockSpec((1, 1, MB_HEADS, MB_HEAD_DIM, tm), lambda b, ti: (b, layer, 0, 0, ti))
    kv_shape = jax.ShapeDtypeStruct((n_batch, depth, MB_HEADS, MB_HEAD_DIM, t), _F32)
    in_specs = _prep_common_specs(n_t, tm)
    n_in = len(in_specs)
    extra, aliases = (), {}
    if kv_prev is not None:
        extra = tuple(kv_prev)
        in_specs = in_specs + [pl.BlockSpec(memory_space=pl.ANY)] * 2
        aliases = {n_in: 1, n_in + 1: 2}
    return pl.pallas_call(
        _mb_prep_t_kernel,
        grid=(n_batch, n_t),
        in_specs=in_specs,
        out_specs=[q_spec, kv_spec, kv_spec],
        out_shape=[jax.ShapeDtypeStruct((n_batch, MB_HEADS, MB_HEAD_DIM, t), _F32), kv_shape, kv_shape],
        input_output_aliases=aliases,
        compiler_params=_cparams(("parallel", "parallel")),
        name="mb_prep_t",
    )(z, z, z, cos_t, sin_t, jnp.tile(qn, MB_HEADS).reshape(1, MB_WIDTH),
      jnp.tile(kn, MB_HEADS).reshape(1, MB_WIDTH), seg, *extra)


def _top3_select(gate, axis):
    n = gate.shape[axis]
    idx = lax.broadcasted_iota(jnp.int32, gate.shape, axis).astype(_F32)
    sel = jnp.zeros(gate.shape, _F32)
    picks = []
    g = gate
    for _ in range(MB_TOPK):
        m = jnp.max(g, axis=axis, keepdims=True)
        first = jnp.min(jnp.where(g == m, idx, float(n)), axis=axis, keepdims=True)
        pick = idx == first
        sel = jnp.where(pick, 1.0, sel)
        picks.append(first)
        g = jnp.where(pick, BELOW_NEG, g)
    return sel, picks


AUG_DEPTH = 2 * MB_HEAD_DIM
V_AUG = LANES
LOG2_E = 1.4426950408889634


def _mb_attn_kernel(qt_ref, kt_ref, vt_ref, o_ref, kt_aug_scr, v_aug_scr, km_scr, s_scr, *, n_blk):
    i = pl.program_id(2)
    heads = range(ATTN_HEADS)
    filler = n_blk
    ext_rows = AUG_DEPTH - MB_HEAD_DIM

    @pl.when(i == 0)
    def _():
        lane = lax.broadcasted_iota(jnp.int32, (MB_HEAD_DIM, LANES), 1)
        ext_row = lax.broadcasted_iota(jnp.int32, (ext_rows, MB_BLOCK), 0)
        ones_row = jnp.where(lax.broadcasted_iota(jnp.int32, (V_AUG - MB_HEAD_DIM, MB_BLOCK), 0) == 0, 1.0, 0.0)
        for h in heads:
            km_t = jnp.zeros((MB_HEAD_DIM, LANES), _F32)
            for j in range(n_blk):
                kt_blk = kt_ref[0, 0, h, :, j * MB_BLOCK:(j + 1) * MB_BLOCK]
                vt_blk = vt_ref[0, 0, h, :, j * MB_BLOCK:(j + 1) * MB_BLOCK]
                kt_aug_scr[h, j] = jnp.concatenate([kt_blk, jnp.where(ext_row == j, 1.0, 0.0)], axis=0).astype(_BF)
                v_aug_scr[h, j] = jnp.concatenate([vt_blk, ones_row], axis=0).T.astype(_BF)
                mean = jnp.sum(kt_blk, axis=1, keepdims=True) * (1.0 / MB_BLOCK)
                km_t = jnp.where(lane == j, mean, km_t)
            km_scr[h] = km_t.T[:n_blk]
            kt_aug_scr[h, filler] = jnp.concatenate(
                [jnp.zeros((MB_HEAD_DIM, MB_BLOCK), _F32), jnp.where(ext_row == filler, 1.0, 0.0)],
                axis=0).astype(_BF)
            v_aug_scr[h, filler] = jnp.zeros((MB_BLOCK, V_AUG), _BF)

    blk_row = lax.broadcasted_iota(jnp.int32, (n_blk, MB_BLOCK), 0)
    row8 = lax.broadcasted_iota(jnp.int32, (SUBLANES, MB_BLOCK), 0)
    q_aug = []
    for h in heads:
        q_t = qt_ref[0, h]
        gate = _dot(km_scr[h], q_t, precision=_HIGHEST)
        eligible = blk_row < i
        sel, _ = _top3_select(jnp.where(eligible, gate, NEG), 0)
        sel = jnp.where(jnp.logical_or(blk_row == i, eligible), jnp.where(blk_row == i, 1.0, sel), 0.0)
        bias = (sel - 1.0) * (-NEG)
        q_s = q_t * (MB_HEAD_DIM ** -0.5 * LOG2_E)
        tail = jnp.where(row8 == 0, NEG, 0.0)
        pad = jnp.zeros((ext_rows - n_blk - SUBLANES, MB_BLOCK), _F32)
        q_aug_t = jnp.concatenate([q_s, bias, tail, pad], axis=0)
        q_aug.append(q_aug_t.T.astype(_BF))

    n_pairs = (i + 1) // 2

    def pair_blocks(jj):
        j0 = 2 * jj
        return (j0, jnp.where(j0 + 1 < i, j0 + 1, filler))

    def fold_max(s):
        return jnp.maximum(s[:, :LANES], s[:, LANES:])

    query_ge_key = (lax.broadcasted_iota(jnp.int32, (MB_BLOCK, MB_BLOCK), 0)
                    >= lax.broadcasted_iota(jnp.int32, (MB_BLOCK, MB_BLOCK), 1))
    m_init = []
    for h in heads:
        s = jnp.where(query_ge_key, _dot(q_aug[h], kt_aug_scr[h, i]), NEG)
        s_scr[h, i] = s
        m_init.append(fold_max(s))

    def score_body(jj, m):
        m = list(m)
        for j in pair_blocks(jj):
            for h in heads:
                s = _dot(q_aug[h], kt_aug_scr[h, j])
                s_scr[h, j] = s
                m[h] = jnp.maximum(m[h], fold_max(s))
        return tuple(m)

    m = lax.fori_loop(0, n_pairs, score_body, tuple(m_init))
    m_b = [jnp.broadcast_to(jnp.max(m[h], axis=1, keepdims=True), (MB_BLOCK, LANES)) for h in heads]

    def weighted_values(h, j):
        s = s_scr[h, j]
        p = jnp.concatenate([jnp.exp2(s[:, :LANES] - m_b[h]), jnp.exp2(s[:, LANES:] - m_b[h])], axis=1)
        return _dot(p.astype(_BF), v_aug_scr[h, j])

    def acc_body(jj, acc):
        acc = list(acc)
        for j in pair_blocks(jj):
            for h in heads:
                acc[h] = acc[h] + weighted_values(h, j)
        return tuple(acc)

    acc = lax.fori_loop(0, n_pairs, acc_body, tuple(weighted_values(h, i) for h in heads))
    o_ref[...] = jnp.concatenate(
        [acc[h][:, :MB_HEAD_DIM] / acc[h][:, MB_HEAD_DIM:MB_HEAD_DIM + 1] for h in heads], axis=1)


def _mb_attn(q_t, k_t_all, v_t_all, layer, n_batch, t):
    n_blk = t // MB_BLOCK
    assert n_blk % SUBLANES == 0 and MB_HEAD_DIM + n_blk + SUBLANES <= AUG_DEPTH
    n_hp = MB_HEADS // ATTN_HEADS
    kern = functools.partial(_mb_attn_kernel, n_blk=n_blk)
    kv_spec = pl.BlockSpec((1, 1, ATTN_HEADS, MB_HEAD_DIM, t), lambda b, hp, i: (b, layer, hp, 0, 0))
    return pl.pallas_call(
        kern,
        grid=(n_batch, n_hp, n_blk),
        in_specs=[pl.BlockSpec((1, ATTN_HEADS, MB_HEAD_DIM, MB_BLOCK), lambda b, hp, i: (b, hp, 0, i)),
                  kv_spec, kv_spec],
        out_specs=pl.BlockSpec((MB_BLOCK, ATTN_HEADS * MB_HEAD_DIM), lambda b, hp, i: (b * n_blk + i, hp)),
        out_shape=jax.ShapeDtypeStruct((n_batch * t, MB_WIDTH), _F32),
        scratch_shapes=[pltpu.VMEM((ATTN_HEADS, n_blk + 1, AUG_DEPTH, MB_BLOCK), _BF),
                        pltpu.VMEM((ATTN_HEADS, n_blk + 1, MB_BLOCK, V_AUG), _BF),
                        pltpu.VMEM((ATTN_HEADS, n_blk, MB_HEAD_DIM), _F32),
                        pltpu.VMEM((ATTN_HEADS, n_blk + 1, MB_BLOCK, MB_BLOCK), _F32)],
        compiler_params=_cparams(("parallel", "parallel", "arbitrary")),
        name="mb_attn",
    )(q_t, k_t_all, v_t_all)


def _kmean_kernel(pt_ref, *refs):
    page_refs, out_ref = refs[:KMEAN_PAGES_PER_STEP], refs[KMEAN_PAGES_PER_STEP]
    for jb in range(KMEAN_PAGES_PER_STEP // PAGES_PER_BLOCK):
        tot = None
        for p in range(PAGES_PER_BLOCK):
            page = page_refs[jb * PAGES_PER_BLOCK + p][0, 0]
            tot = page if tot is None else tot + page
        out_ref[0, 0, 0, :, :, jb:jb + 1] = jnp.sum(tot, axis=-1, keepdims=True) * (1.0 / MB_BLOCK)


def _kmean(cache_kt, pt_flat, n_layers, n_batch, n_pages):
    n_steps = n_pages // KMEAN_PAGES_PER_STEP
    blocks_per_step = KMEAN_PAGES_PER_STEP // PAGES_PER_BLOCK

    def page_spec(p):
        return pl.BlockSpec((1, 1, MB_HEADS, MB_HEAD_DIM, PAGE_SIZE),
                            lambda l, b, c, pt, p=p: (pt[b * n_pages + c * KMEAN_PAGES_PER_STEP + p], l, 0, 0, 0))

    grid_spec = pltpu.PrefetchScalarGridSpec(
        num_scalar_prefetch=1,
        grid=(n_layers, n_batch, n_steps),
        in_specs=[page_spec(p) for p in range(KMEAN_PAGES_PER_STEP)],
        out_specs=pl.BlockSpec((1, 1, 1, MB_HEADS, MB_HEAD_DIM, blocks_per_step),
                               lambda l, b, c, pt: (l, b, c, 0, 0, 0)),
    )
    return pl.pallas_call(
        _kmean_kernel,
        grid_spec=grid_spec,
        out_shape=jax.ShapeDtypeStruct((n_layers, n_batch, n_steps, MB_HEADS, MB_HEAD_DIM, blocks_per_step), _F32),
        compiler_params=_cparams(("parallel", "parallel", "parallel")),
        name="kmean",
    )(pt_flat, *([cache_kt] * KMEAN_PAGES_PER_STEP))


GATE_PAGE_LANE0 = SUBLANES


def _gate_kernel(q_ref, km_ref, pt_ref, idx_ref, *, own):
    n_past = km_ref.shape[3]
    n_pages = pt_ref.shape[2]
    col = lax.broadcasted_iota(jnp.int32, (SAMPLE_T_PAD, n_past), 1)
    lane = lax.broadcasted_iota(jnp.int32, (SAMPLE_T_PAD, LANES), 1)
    page_col = lax.broadcasted_iota(jnp.int32, (SAMPLE_T_PAD, n_pages), 1).astype(_F32)
    pt_row = pt_ref[0].astype(_F32)
    for h in range(MB_HEADS):
        gate = _dot(q_ref[0, h], km_ref[0, h], precision=_HIGHEST)
        _, picks = _top3_select(jnp.where(col < own, gate, NEG), 1)
        out = jnp.zeros((SAMPLE_T_PAD, LANES), jnp.int32)
        for r, first in enumerate(picks):
            out = jnp.where(lane == r, first.astype(jnp.int32), out)
            for p in range(PAGES_PER_BLOCK):
                logical = first * PAGES_PER_BLOCK + p
                phys = jnp.sum(jnp.where(page_col == logical, pt_row, 0.0), axis=1, keepdims=True)
                out = jnp.where(lane == GATE_PAGE_LANE0 + r * PAGES_PER_BLOCK + p, phys.astype(jnp.int32), out)
        idx_ref[0, h] = out


def _gate(q_s, kmean_l, page_table, own):
    n_batch = q_s.shape[0]
    n_past = kmean_l.shape[3]
    n_pages = page_table.shape[1]
    return pl.pallas_call(
        functools.partial(_gate_kernel, own=own),
        grid=(n_batch,),
        in_specs=[pl.BlockSpec((1, MB_HEADS, SAMPLE_T_PAD, MB_HEAD_DIM), lambda b: (b, 0, 0, 0)),
                  pl.BlockSpec((1, MB_HEADS, MB_HEAD_DIM, n_past), lambda b: (b, 0, 0, 0)),
                  pl.BlockSpec((1, 1, n_pages), lambda b: (b, 0, 0))],
        out_specs=pl.BlockSpec((1, MB_HEADS, SAMPLE_T_PAD, LANES), lambda b: (b, 0, 0, 0)),
        out_shape=jax.ShapeDtypeStruct((n_batch, MB_HEADS, SAMPLE_T_PAD, LANES), jnp.int32),
        compiler_params=_cparams(("parallel",)),
        name="mb_gate",
    )(q_s, kmean_l, page_table.reshape(n_batch, 1, n_pages))


def _sample_attn_kernel(pg_ref, idx_ref, q_ref, kn_ref, vn_ref, ck_ref, cv_ref, o_ref, kbuf, vbuf, sem,
                        *, t_s, own, layer):
    n_slots = t_s * MB_TOPK * PAGES_PER_BLOCK
    b = pl.program_id(0)
    row = lax.broadcasted_iota(jnp.int32, (SAMPLE_T_PAD, 1), 0)

    def page_copies(h, par):
        base = (b * MB_HEADS + h) * n_slots
        out = []
        for n in range(n_slots):
            page = pg_ref[base + n]
            out.append(pltpu.make_async_copy(ck_ref.at[page, layer, h], kbuf.at[par, n], sem.at[0, par]))
            out.append(pltpu.make_async_copy(cv_ref.at[page, layer, h], vbuf.at[par, n], sem.at[1, par]))
        return out

    for cp in page_copies(0, 0):
        cp.start()

    def head_body(h, carry):
        par = h % 2

        @pl.when(h + 1 < MB_HEADS)
        def _():
            for cp in page_copies(h + 1, 1 - par):
                cp.start()

        for cp in page_copies(h, par):
            cp.wait()

        q = q_ref[0, h] * (MB_HEAD_DIM ** -0.5)
        q_bf = q.astype(_BF)
        scores = []
        for t in range(t_s):
            for j in range(MB_TOPK):
                blk = idx_ref[((b * MB_HEADS + h) * t_s + t) * MB_TOPK + j]
                allowed = row == jnp.where(blk < own, t, -1)
                for p in range(PAGES_PER_BLOCK):
                    slot = (t * MB_TOPK + j) * PAGES_PER_BLOCK + p
                    s = _dot(q_bf, kbuf[par, slot].astype(_BF))
                    scores.append(jnp.where(allowed, s, NEG))
        kn = kn_ref[0, h]
        vn = vn_ref[0, h]
        own_scores = []
        for c in range(t_s):
            s = jnp.sum(q * kn[c:c + 1, :], axis=1, keepdims=True)
            own_scores.append(jnp.where(row >= c, s, NEG))

        m = own_scores[0]
        for s in own_scores[1:]:
            m = jnp.maximum(m, s)
        for s in scores:
            m = jnp.maximum(m, jnp.max(s, axis=1, keepdims=True))
        l = jnp.zeros((SAMPLE_T_PAD, 1), _F32)
        acc = jnp.zeros((SAMPLE_T_PAD, MB_HEAD_DIM), _F32)
        for c, s in enumerate(own_scores):
            p = jnp.exp(s - m)
            l = l + p
            acc = acc + p * vn[c:c + 1, :]
        for slot, s in enumerate(scores):
            p = jnp.exp(s - m)
            l = l + jnp.sum(p, axis=1, keepdims=True)
            acc = acc + _dot_nt(p.astype(_BF), vbuf[par, slot].astype(_BF))
        o_ref[0, h] = acc / l
        return carry

    lax.fori_loop(0, MB_HEADS, head_body, 0)


def _sample_attn(q_s, k_s, v_s, cache_kt, cache_vt, pages_flat, idx_flat, layer, t_s, own):
    n_batch = q_s.shape[0]
    n_slots = t_s * MB_TOPK * PAGES_PER_BLOCK
    new_spec = pl.BlockSpec((1, MB_HEADS, SAMPLE_T_PAD, MB_HEAD_DIM), lambda b, pg, idx: (b, 0, 0, 0))
    hbm = pl.BlockSpec(memory_space=pl.ANY)
    grid_spec = pltpu.PrefetchScalarGridSpec(
        num_scalar_prefetch=2,
        grid=(n_batch,),
        in_specs=[new_spec, new_spec, new_spec, hbm, hbm],
        out_specs=new_spec,
        scratch_shapes=[pltpu.VMEM((2, n_slots, MB_HEAD_DIM, PAGE_SIZE), _F32),
                        pltpu.VMEM((2, n_slots, MB_HEAD_DIM, PAGE_SIZE), _F32),
                        pltpu.SemaphoreType.DMA((2, 2))],
    )
    return pl.pallas_call(
        functools.partial(_sample_attn_kernel, t_s=t_s, own=own, layer=layer),
        grid_spec=grid_spec,
        out_shape=jax.ShapeDtypeStruct((n_batch, MB_HEADS, SAMPLE_T_PAD, MB_HEAD_DIM), _F32),
        compiler_params=_cparams(("parallel",)),
        name="mb_sample_attn",
    )(pages_flat, idx_flat, q_s, k_s, v_s, cache_kt, cache_vt)


def _pool_kernel(u_ref, hist_ref, w_ref, sc_ref, y_ref, hist_out_ref, ext_scr, *, tm, pos0, t_last):
    ti = pl.program_id(1)

    @pl.when(ti == 0)
    def _():
        ext_scr[0:POOL_HALO, :] = hist_ref[0]

    u = u_ref[...].astype(_F32)
    ext_scr[POOL_HALO:POOL_HALO + tm, :] = u
    pos = pos0 + ti * tm + lax.broadcasted_iota(jnp.int32, (tm, 1), 0)
    for g, w in enumerate(POOL_WINDOWS):
        sl = slice(g * POOL_GROUP_DIM, (g + 1) * POOL_GROUP_DIM)
        tot = u[:, sl]
        for d in range(1, w):
            tot = tot + ext_scr[POOL_HALO - d:POOL_HALO - d + tm, sl]
        cnt = jnp.minimum(pos + 1, w).astype(_F32)
        diff = tot / cnt - u[:, sl]
        y = _dot(diff.astype(_BF), w_ref[g])
        y_ref[:, sl] = y * sc_ref[:, sl]

    @pl.when(ti == pl.num_programs(1) - 1)
    def _():
        hist_out_ref[0] = ext_scr[t_last:t_last + POOL_HALO, :]

    @pl.when(ti < pl.num_programs(1) - 1)
    def _():
        ext_scr[0:POOL_HALO, :] = ext_scr[tm:tm + POOL_HALO, :]


def _pool(z, hist16, pool_w_bf, pool_scale, n_batch, t_pad, tm, pos0, t_valid):
    n_t = t_pad // tm
    t_last = t_valid - (n_t - 1) * tm
    hist_spec = pl.BlockSpec((1, POOL_HALO, POOL_WIDTH), lambda b, ti: (b, 0, 0))
    kern = functools.partial(_pool_kernel, tm=tm, pos0=pos0, t_last=t_last)
    return pl.pallas_call(
        kern,
        grid=(n_batch, n_t),
        in_specs=[pl.BlockSpec((tm, POOL_WIDTH), lambda b, ti: (b * n_t + ti, COL_UC)),
                  hist_spec,
                  pl.BlockSpec((POOL_GROUPS, POOL_GROUP_DIM, POOL_GROUP_DIM), lambda b, ti: (0, 0, 0)),
                  pl.BlockSpec((1, POOL_WIDTH), lambda b, ti: (0, 0))],
        out_specs=[pl.BlockSpec((tm, POOL_WIDTH), lambda b, ti: (b * n_t + ti, 0)), hist_spec],
        out_shape=[jax.ShapeDtypeStruct((n_batch * t_pad, POOL_WIDTH), _F32),
                   jax.ShapeDtypeStruct((n_batch, POOL_HALO, POOL_WIDTH), _F32)],
        scratch_shapes=[pltpu.VMEM((POOL_HALO + tm, POOL_WIDTH), _F32)],
        compiler_params=_cparams(("parallel", "arbitrary")),
        name="pool",
    )(z, hist16, pool_w_bf, pool_scale.reshape(1, POOL_WIDTH))


def _mix_kernel(x_ref, a_ref, b_ref, c_ref, g0_ref, g1_ref, g2_ref, wb_ref, wo_ref, gn_ref, w1_ref, w2_ref, o_ref):
    merged = None
    for br_ref, g_ref, n in ((a_ref, g0_ref, 0), (b_ref, g1_ref, 1), (c_ref, g2_ref, 2)):
        proj = _dot(br_ref[...].astype(_BF), wb_ref[n])
        term = _sigmoid(g_ref[...].astype(_F32)) * proj
        merged = term if merged is None else merged + term
    x = x_ref[...] + _dot(merged.astype(_BF), wo_ref[...])
    ms = jnp.mean(x * x, axis=-1, keepdims=True)
    h = (x * lax.rsqrt(ms + EPS) * gn_ref[...]).astype(_BF)
    hid = jnp.maximum(_dot(h, w1_ref[...]), 0.0)
    hid = (hid * hid).astype(_BF)
    o_ref[...] = x + _dot(hid, w2_ref[...])


def _mix(x, a, b, c, z, wb_bf, wo_bf, g_ffn, w1_bf, w2_bf, tm):
    n = x.shape[0]
    row = lambda w: pl.BlockSpec((tm, w), lambda i: (i, 0))
    gate = lambda k: pl.BlockSpec((tm, D_MODEL), lambda i, k=k: (i, COL_GATE_1024 + k))
    const = lambda shape: pl.BlockSpec(shape, lambda i: (0,) * len(shape), pipeline_mode=pl.Buffered(1))
    return pl.pallas_call(
        _mix_kernel,
        grid=(n // tm,),
        in_specs=[row(D_MODEL), row(BRANCH_WIDTH), row(BRANCH_WIDTH), row(BRANCH_WIDTH),
                  gate(0), gate(1), gate(2),
                  const((N_BRANCH, BRANCH_WIDTH, D_MODEL)), const((D_MODEL, D_MODEL)),
                  const((1, D_MODEL)), const((D_MODEL, D_FF)), const((D_FF, D_MODEL))],
        out_specs=row(D_MODEL),
        out_shape=jax.ShapeDtypeStruct((n, D_MODEL), _F32),
        compiler_params=_cparams(("parallel",)),
        name="mix",
    )(x, a, b, c, z, z, z, wb_bf, wo_bf, g_ffn.reshape(1, D_MODEL), w1_bf, w2_bf)


def _rope_tables(pos):
    half = MB_HEAD_DIM // 2
    inv = ROPE_THETA ** (-jnp.arange(half, dtype=_F32) / half)
    ang = pos.astype(_F32)[:, None] * inv[None, :]
    cos, sin = jnp.cos(ang), jnp.sin(ang)
    cos_t = jnp.concatenate([cos, cos, cos, cos], axis=1)
    sin_t = jnp.concatenate([-sin, sin, -sin, sin], axis=1)
    return cos_t, sin_t


def _tile_rows(n, cap):
    t = cap
    while n % t:
        t //= 2
    return t


def _layer(x, *, n_batch, t_pad, t_valid, pos0, chunk, sub, lb, s0_t, hist16, moba, w):
    n = x.shape[0]
    z_dtype = _BF if t_pad % (2 * SUBLANES) == 0 else _F32
    z = _inproj(x, w["norm_mix"], w["w_in"], _tile_rows(n, 1024), D_IN // 4, z_dtype)
    a_out, s_new_t = _hgrn(z, lb, w["hg_onorm"], s0_t, n_batch, t_pad, chunk, sub, t_valid)
    b_out, kv = moba(z)
    c_out, hist_new = _pool(z, hist16, w["pool_w"], w["pool_scale"], n_batch, t_pad, _tile_rows(t_pad, 512),
                            pos0, t_valid)
    x2 = _mix(x, a_out, b_out, c_out, z, w["w_branch"], w["w_out"], w["norm_ffn"], w["w_ff1"], w["w_ff2"],
              _tile_rows(n, 256))
    return x2, kv, s_new_t, hist_new


def kernel(x_prompt, x_sample, cache_k, cache_v, state_hgrn, state_pool, page_table, norm_mix, w_in, hg_lb,
           hg_onorm, mb_qnorm, mb_knorm, pool_w, pool_scale, w_branch, w_out, norm_ffn, w_ff1, w_ff2):
    n_b, t_p, _ = x_prompt.shape
    n_db, t_s, _ = x_sample.shape
    depth = w_in.shape[0]
    n_pages = page_table.shape[1]
    past_len = n_pages * PAGE_SIZE
    assert t_p % MB_BLOCK == 0 and t_p % HG_CHUNK == 0
    assert past_len % MB_BLOCK == 0 and t_s <= SAMPLE_T_PAD and n_pages % KMEAN_PAGES_PER_STEP == 0
    own_s = past_len // MB_BLOCK

    lb_soft = jax.nn.softmax(hg_lb.astype(_F32), axis=0)
    lb_all = jnp.cumsum(lb_soft, axis=0) - lb_soft[0:1]

    seg = (jnp.arange(MB_WIDTH)[:, None] // MB_HEAD_DIM == jnp.arange(MB_WIDTH)[None, :] // MB_HEAD_DIM).astype(_BF)
    cos_p, sin_p = _rope_tables(jnp.arange(t_p, dtype=jnp.int32))
    cos_s, sin_s = _rope_tables(past_len + jnp.arange(SAMPLE_T_PAD, dtype=jnp.int32))

    cache_kt = jnp.swapaxes(cache_k, 3, 4)
    cache_vt = jnp.swapaxes(cache_v, 3, 4)
    pt_flat = page_table.reshape(-1).astype(jnp.int32)
    kmean_all = _kmean(cache_kt, pt_flat, depth, n_db, n_pages)
    kmean_all = jnp.transpose(kmean_all, (0, 1, 3, 4, 2, 5)).reshape(depth, n_db, MB_HEADS, MB_HEAD_DIM, own_s)

    xp = x_prompt.reshape(n_b * t_p, D_MODEL)
    xs = jnp.pad(x_sample, ((0, 0), (0, SAMPLE_T_PAD - t_s), (0, 0))).reshape(n_db * SAMPLE_T_PAD, D_MODEL)

    s0_p = jnp.zeros((n_b, HG_HEADS, HG_DV, HG_DK), _F32)
    hist_p = jnp.zeros((n_b, POOL_HALO, POOL_WIDTH), _F32)

    kv_p = None
    ks_l, vs_l, sp_l, ss_l, hp_l, hs_l = [], [], [], [], [], []
    for l in range(depth):
        w = dict(norm_mix=norm_mix[l], w_in=w_in[l].astype(_BF), hg_onorm=hg_onorm[l], mb_qnorm=mb_qnorm[l],
                 mb_knorm=mb_knorm[l], pool_w=pool_w[l].astype(_BF), pool_scale=pool_scale[l],
                 w_branch=w_branch[l].astype(_BF), w_out=w_out[l].astype(_BF), norm_ffn=norm_ffn[l],
                 w_ff1=w_ff1[l].astype(_BF), w_ff2=w_ff2[l].astype(_BF))

        def moba_p(z, l=l, w=w, kv_prev=kv_p):
            q_t, k_t_all, v_t_all = _mb_prep_t(z, cos_p, sin_p, w["mb_qnorm"], w["mb_knorm"], seg, n_b, t_p,
                                               _tile_rows(t_p, 512), l, depth, kv_prev)
            return _mb_attn(q_t, k_t_all, v_t_all, l, n_b, t_p), (k_t_all, v_t_all)

        xp, kv_p, sp_t, hp = _layer(
            xp, n_batch=n_b, t_pad=t_p, t_valid=t_p, pos0=0, chunk=HG_CHUNK, sub=HG_SUB, lb=lb_all[l],
            s0_t=s0_p, hist16=hist_p, moba=moba_p, w=w)

        def moba_s(z, l=l, w=w):
            q, k, v = _mb_prep(z, cos_s, sin_s, w["mb_qnorm"], w["mb_knorm"], seg, n_db, SAMPLE_T_PAD, SAMPLE_T_PAD)
            idx = _gate(q, kmean_all[l], page_table, own_s)
            idx_flat = idx[:, :, :t_s, :MB_TOPK].reshape(-1)
            pages_flat = idx[:, :, :t_s, GATE_PAGE_LANE0:GATE_PAGE_LANE0 + MB_TOPK * PAGES_PER_BLOCK].reshape(-1)
            o = _sample_attn(q, k, v, cache_kt, cache_vt, pages_flat, idx_flat, l, t_s, own_s)
            return jnp.transpose(o, (0, 2, 1, 3)).reshape(n_db * SAMPLE_T_PAD, MB_WIDTH), (k, v)

        s0_s = jnp.swapaxes(state_hgrn[l], -1, -2)
        hist_s = jnp.pad(state_pool[l], ((0, 0), (POOL_HALO - POOL_HIST, 0), (0, 0)))
        xs, (ks, vs), ss_t, hs = _layer(
            xs, n_batch=n_db, t_pad=SAMPLE_T_PAD, t_valid=t_s, pos0=past_len, chunk=SAMPLE_T_PAD,
            sub=SAMPLE_T_PAD, lb=lb_all[l], s0_t=s0_s, hist16=hist_s, moba=moba_s, w=w)

        ks_l.append(ks[:, :, :t_s]); vs_l.append(vs[:, :, :t_s])
        sp_l.append(jnp.swapaxes(sp_t, -1, -2)); ss_l.append(jnp.swapaxes(ss_t, -1, -2))
        hp_l.append(hp[:, POOL_HALO - POOL_HIST:]); hs_l.append(hs[:, POOL_HALO - POOL_HIST:])

    y_p = xp.reshape(n_b, t_p, D_MODEL)
    y_s = xs.reshape(n_db, SAMPLE_T_PAD, D_MODEL)[:, :t_s]
    k_p = jnp.swapaxes(kv_p[0], 3, 4)
    v_p = jnp.swapaxes(kv_p[1], 3, 4)
    return (y_p, y_s, k_p, v_p, jnp.stack(ks_l, axis=1), jnp.stack(vs_l, axis=1),
            jnp.stack(sp_l, axis=0), jnp.stack(ss_l, axis=0), jnp.stack(hp_l, axis=0), jnp.stack(hs_l, axis=0))
```

```python
import functools

import jax
import jax.numpy as jnp
from jax import lax
from jax.experimental import pallas as pl
from jax.experimental.pallas import tpu as pltpu

D_MODEL = 1024
PAGE_SIZE = 128
HG_HEADS = 4
HG_DK = 128
HG_DV = 128
HG_WIDTH = HG_HEADS * HG_DV
HG_CHUNK = 64
HG_SUB = 16
HG_SEQS_PER_STEP = 2
MB_HEADS = 8
MB_HEAD_DIM = 64
MB_WIDTH = MB_HEADS * MB_HEAD_DIM
MB_BLOCK = 256
MB_TOPK = 3
ROPE_THETA = 10000.0
POOL_WINDOWS = (2, 4, 8, 16)
POOL_GROUPS = 4
POOL_GROUP_DIM = 128
POOL_WIDTH = POOL_GROUPS * POOL_GROUP_DIM
POOL_HIST = 15
POOL_HALO = 16
N_BRANCH = 3
BRANCH_WIDTH = 512
D_FF = 4 * D_MODEL
EPS = 1e-6
NEG = -1e30
BELOW_NEG = -3e38
F_FLOOR = 1e-30
D_IN = 4 * HG_WIDTH + 3 * MB_WIDTH + POOL_WIDTH + N_BRANCH * D_MODEL
COL_QA, COL_FA, COL_IA, COL_GA, COL_QB, COL_KB, COL_VB, COL_UC = range(8)
COL_GATE_1024 = 4

LANES = 128
SUBLANES = 8
SAMPLE_T_PAD = SUBLANES
VMEM_LIMIT = 56 * 1024 * 1024
PAGES_PER_BLOCK = MB_BLOCK // PAGE_SIZE
KMEAN_PAGES_PER_STEP = 64
HEADS_PER_STEP = LANES // MB_HEAD_DIM
ATTN_HEADS = 4
ATTN_BLOCKS_PER_TRIP = 3

_BF = jnp.bfloat16
_F32 = jnp.float32
_HIGHEST = lax.Precision.HIGHEST


def _cparams(sem, vmem=VMEM_LIMIT):
    return pltpu.CompilerParams(dimension_semantics=sem, vmem_limit_bytes=vmem)


def _dot(a, b, precision=None):
    return jnp.dot(a, b, preferred_element_type=_F32, precision=precision)


def _dot_nt(a, b, precision=None):
    return lax.dot_general(a, b, (((1,), (1,)), ((), ())), preferred_element_type=_F32, precision=precision)


def _dot_tn(a, b):
    return lax.dot_general(a, b, (((0,), (0,)), ((), ())), preferred_element_type=_F32)


def _sigmoid(x):
    return 1.0 / (1.0 + jnp.exp(-x))


def _inproj_kernel(x_ref, g_ref, w_ref, z_ref, h_ref):
    @pl.when(pl.program_id(1) == 0)
    def _():
        x = x_ref[...]
        ms = jnp.mean(x * x, axis=-1, keepdims=True)
        h_ref[...] = (x * lax.rsqrt(ms + EPS) * g_ref[...]).astype(_BF)

    z_ref[...] = _dot(h_ref[...], w_ref[...]).astype(z_ref.dtype)


def _inproj(x, g, w_bf, tm, tn, z_dtype):
    n = x.shape[0]
    return pl.pallas_call(
        _inproj_kernel,
        grid=(n // tm, D_IN // tn),
        in_specs=[
            pl.BlockSpec((tm, D_MODEL), lambda i, j: (i, 0)),
            pl.BlockSpec((1, D_MODEL), lambda i, j: (0, 0)),
            pl.BlockSpec((D_MODEL, tn), lambda i, j: (0, j)),
        ],
        out_specs=pl.BlockSpec((tm, tn), lambda i, j: (i, j)),
        out_shape=jax.ShapeDtypeStruct((n, D_IN), z_dtype),
        scratch_shapes=[pltpu.VMEM((tm, D_MODEL), _BF)],
        compiler_params=_cparams(("parallel", "arbitrary")),
        name="inproj",
    )(x, g.reshape(1, D_MODEL), w_bf)


def _hgrn_kernel(q_ref, f_ref, i_ref, g_ref, lb_ref, on_ref, s0_ref, a_ref, s_out_ref,
                 st_scr, b_scr, k_scr, v_scr, *, chunk, cps, sub, t_valid):
    ci = pl.program_id(1)

    @pl.when(ci == 0)
    def _():
        for cc in range(cps):
            for h in range(HG_HEADS):
                st_scr[cc * HG_HEADS + h] = s0_ref[cc, h]

    row = lax.broadcasted_iota(jnp.int32, (chunk, 1), 0)
    row8 = lax.broadcasted_iota(jnp.int32, (SUBLANES, 1), 0)
    tril = (lax.broadcasted_iota(jnp.int32, (chunk, chunk), 0)
            >= lax.broadcasted_iota(jnp.int32, (chunk, chunk), 1)).astype(_F32)
    n_sub = chunk // sub
    heads = range(HG_HEADS)
    lanes = [slice(h * HG_DK, (h + 1) * HG_DK) for h in heads]
    units = [(h, cc) for cc in range(cps) for h in heads]
    q_u, k_u, vbf_u, b_u = {}, {}, {}, {}

    for h, cc in units:
        sl, hc = lanes[h], h * cps + cc
        qr = q_ref[cc, :, sl].astype(_F32)
        fr = f_ref[cc, :, sl].astype(_F32)
        v = i_ref[cc, :, sl].astype(_F32)
        lb = lb_ref[:, sl]
        fval = lb + (1.0 - lb) * _sigmoid(fr)
        log2f = jnp.log2(jnp.maximum(fval, F_FLOOR))
        k = (1.0 - lb) * _sigmoid(-fr)
        if t_valid < chunk:
            valid = row < t_valid
            log2f = jnp.where(valid, log2f, 0.0)
            k = jnp.where(valid, k, 0.0)
        b = _dot(tril, log2f, precision=_HIGHEST)
        b_scr[hc] = b
        k_scr[hc] = k
        v_scr[hc] = v
        q_u[h, cc], k_u[h, cc], vbf_u[h, cc], b_u[h, cc] = qr * _sigmoid(qr), k, v.astype(_BF), b

    st = {(h, cc): st_scr[cc * HG_HEADS + h] for h, cc in units}
    acc = {}
    for cc in range(cps):
        for h in heads:
            q, k, v_bf, b = q_u[h, cc], k_u[h, cc], vbf_u[h, cc], b_u[h, cc]
            o = _dot_nt((q * jnp.exp2(b)).astype(_BF), st[h, cc].astype(_BF))
            b_last = b[chunk - 1:chunk]
            k_end = (k * jnp.exp2(b_last - b)).astype(_BF)
            st[h, cc] = st[h, cc] * jnp.exp2(b_last) + _dot_tn(v_bf, k_end)
            for si in range(n_sub):
                r0 = si * sub
                o_s = o[r0:r0 + sub]
                if si > 0:
                    ref = b[r0 - 1:r0]
                    q_dec = (q[r0:r0 + sub] * jnp.exp2(b[r0:r0 + sub] - ref)).astype(_BF)
                    k_dec = (k[:r0] * jnp.exp2(ref - b[:r0])).astype(_BF)
                    att = _dot_nt(q_dec, k_dec)
                    o_s = o_s + _dot(att.astype(_BF), v_bf[:r0])
                for g0 in range(0, sub, SUBLANES):
                    acc[h, cc, r0 + g0] = o_s[g0:g0 + SUBLANES]
    for h, cc in units:
        st_scr[cc * HG_HEADS + h] = st[h, cc]

    for si in range(n_sub):
        r0 = si * sub
        for g0 in range(0, sub, SUBLANES):
            rg = slice(r0 + g0, r0 + g0 + SUBLANES)
            for s in range(g0 + SUBLANES):
                rs = slice(r0 + s, r0 + s + 1)
                for h, cc in units:
                    hc = h * cps + cc
                    d = b_u[h, cc][rg] - b_scr[hc, rs, :]
                    if s > g0:
                        d = jnp.where(row8 >= s - g0, d, NEG)
                    p = q_u[h, cc][rg] * jnp.exp2(d) * k_scr[hc, rs, :]
                    acc[h, cc, r0 + g0] = acc[h, cc, r0 + g0] + jnp.sum(p, axis=-1, keepdims=True) * v_scr[hc, rs, :]

    for h, cc in units:
        sl = lanes[h]
        o = jnp.concatenate([acc[h, cc, r] for r in range(0, chunk, SUBLANES)], axis=0) if chunk > SUBLANES \
            else acc[h, cc, 0]
        gr = g_ref[cc, :, sl].astype(_F32)
        ms = jnp.mean(o * o, axis=-1, keepdims=True)
        a_ref[cc, :, sl] = o * lax.rsqrt(ms + EPS) * on_ref[...] * (gr * _sigmoid(gr))

    @pl.when(ci == pl.num_programs(1) - 1)
    def _():
        for cc in range(cps):
            for h in range(HG_HEADS):
                s_out_ref[cc, h] = st_scr[cc * HG_HEADS + h]


def _hgrn(z, lb, onorm, s0_t, n_batch, t_pad, chunk, sub, t_valid):
    cps = HG_SEQS_PER_STEP
    assert n_batch % cps == 0 and t_pad % chunk == 0
    n_chunks = t_pad // chunk
    z3 = z.reshape(n_batch, t_pad, D_IN)
    col = lambda c: pl.BlockSpec((cps, chunk, HG_WIDTH), lambda b, ci, c=c: (b, ci, c))
    state_spec = pl.BlockSpec((cps, HG_HEADS, HG_DV, HG_DK), lambda b, ci: (b, 0, 0, 0))
    kern = functools.partial(_hgrn_kernel, chunk=chunk, cps=cps, sub=sub, t_valid=t_valid)
    a_out, s_new = pl.pallas_call(
        kern,
        grid=(n_batch // cps, n_chunks),
        in_specs=[col(COL_QA), col(COL_FA), col(COL_IA), col(COL_GA),
                  pl.BlockSpec((1, HG_WIDTH), lambda b, ci: (0, 0)),
                  pl.BlockSpec((1, HG_DV), lambda b, ci: (0, 0)),
                  state_spec],
        out_specs=[pl.BlockSpec((cps, chunk, HG_WIDTH), lambda b, ci: (b, ci, 0)), state_spec],
        out_shape=[jax.ShapeDtypeStruct((n_batch, t_pad, HG_WIDTH), _F32),
                   jax.ShapeDtypeStruct((n_batch, HG_HEADS, HG_DV, HG_DK), _F32)],
        scratch_shapes=[pltpu.VMEM((HG_HEADS * cps, HG_DV, HG_DK), _F32),
                        pltpu.VMEM((HG_HEADS * cps, chunk, HG_DK), _F32),
                        pltpu.VMEM((HG_HEADS * cps, chunk, HG_DK), _F32),
                        pltpu.VMEM((HG_HEADS * cps, chunk, HG_DV), _F32)],
        compiler_params=_cparams(("parallel", "arbitrary")),
        name="hgrn",
    )(z3, z3, z3, z3, lb.reshape(1, HG_WIDTH), onorm.reshape(1, HG_DV), s0_t)
    return a_out.reshape(n_batch * t_pad, HG_WIDTH), s_new


def _norm_rope(x, w, cos, sin, seg):
    lane = lax.broadcasted_iota(jnp.int32, (1, LANES), 1)
    first_half = (lane % MB_HEAD_DIM) < (MB_HEAD_DIM // 2)
    sq = x * x
    hi = sq.astype(_BF)
    lo = (sq - hi.astype(_F32)).astype(_BF)
    ss = _dot(hi, seg) + _dot(lo, seg)
    xn = x * lax.rsqrt(ss * (1.0 / MB_HEAD_DIM) + EPS) * w
    outs = []
    for g in range(MB_WIDTH // LANES):
        xg = xn[:, g * LANES:(g + 1) * LANES]
        rot = jnp.where(first_half, pltpu.roll(xg, LANES - MB_HEAD_DIM // 2, 1),
                        pltpu.roll(xg, MB_HEAD_DIM // 2, 1))
        outs.append(xg * cos + rot * sin)
    return outs


def _mb_prep_kernel(q_ref, k_ref, v_ref, cos_ref, sin_ref, qn_ref, kn_ref, seg_ref,
                    qo_ref, ko_ref, vo_ref):
    q_g = _norm_rope(q_ref[...].astype(_F32), qn_ref[...], cos_ref[...], sin_ref[...], seg_ref[...])
    k_g = _norm_rope(k_ref[...].astype(_F32), kn_ref[...], cos_ref[...], sin_ref[...], seg_ref[...])
    v = v_ref[...].astype(_F32)
    for h in range(MB_HEADS):
        g, off = divmod(h * MB_HEAD_DIM, LANES)
        qo_ref[0, h] = q_g[g][:, off:off + MB_HEAD_DIM]
        ko_ref[0, h] = k_g[g][:, off:off + MB_HEAD_DIM]
        vo_ref[0, h] = v[:, h * MB_HEAD_DIM:(h + 1) * MB_HEAD_DIM]


def _mb_prep_t_kernel(q_ref, k_ref, v_ref, cos_ref, sin_ref, qn_ref, kn_ref, seg_ref, *rest):
    qo_ref, ko_ref, vo_ref = rest[-3:]
    q_g = _norm_rope(q_ref[...].astype(_F32), qn_ref[...], cos_ref[...], sin_ref[...], seg_ref[...])
    k_g = _norm_rope(k_ref[...].astype(_F32), kn_ref[...], cos_ref[...], sin_ref[...], seg_ref[...])
    v = v_ref[...].astype(_F32)
    for g in range(MB_WIDTH // LANES):
        q_t = q_g[g].T
        k_t = k_g[g].T
        v_t = v[:, g * LANES:(g + 1) * LANES].T
        for hh in range(HEADS_PER_STEP):
            h = g * HEADS_PER_STEP + hh
            rows = slice(hh * MB_HEAD_DIM, (hh + 1) * MB_HEAD_DIM)
            qo_ref[0, h] = q_t[rows]
            ko_ref[0, 0, h] = k_t[rows]
            vo_ref[0, 0, h] = v_t[rows]


def _prep_common_specs(n_t, tm):
    col = lambda c: pl.BlockSpec((tm, MB_WIDTH), lambda b, ti, c=c: (b * n_t + ti, c))
    tab = pl.BlockSpec((tm, LANES), lambda b, ti: (ti, 0))
    vec = pl.BlockSpec((1, MB_WIDTH), lambda b, ti: (0, 0))
    return [col(COL_QB), col(COL_KB), col(COL_VB), tab, tab, vec, vec,
            pl.BlockSpec((MB_WIDTH, MB_WIDTH), lambda b, ti: (0, 0))]


def _mb_prep(z, cos_t, sin_t, qn, kn, seg, n_batch, t_pad, tm):
    n_t = t_pad // tm
    out = pl.BlockSpec((1, MB_HEADS, tm, MB_HEAD_DIM), lambda b, ti: (b, 0, ti, 0))
    shp = jax.ShapeDtypeStruct((n_batch, MB_HEADS, t_pad, MB_HEAD_DIM), _F32)
    return pl.pallas_call(
        _mb_prep_kernel,
        grid=(n_batch, n_t),
        in_specs=_prep_common_specs(n_t, tm),
        out_specs=[out, out, out],
        out_shape=[shp, shp, shp],
        compiler_params=_cparams(("parallel", "parallel")),
        name="mb_prep",
    )(z, z, z, cos_t, sin_t, jnp.tile(qn, MB_HEADS).reshape(1, MB_WIDTH),
      jnp.tile(kn, MB_HEADS).reshape(1, MB_WIDTH), seg)


def _mb_prep_t(z, cos_t, sin_t, qn, kn, seg, n_batch, t, tm, layer, depth, kv_prev):
    n_t = t // tm
    q_spec = pl.BlockSpec((1, MB_HEADS, MB_HEAD_DIM, tm), lambda b, ti: (b, 0, 0, ti))
    kv_spec = pl.BlockSpec((1, 1, MB_HEADS, MB_HEAD_DIM, tm), lambda b, ti: (b, layer, 0, 0, ti))
    kv_shape = jax.ShapeDtypeStruct((n_batch, depth, MB_HEADS, MB_HEAD_DIM, t), _F32)
    in_specs = _prep_common_specs(n_t, tm)
    n_in = len(in_specs)
    extra, aliases = (), {}
    if kv_prev is not None:
        extra = tuple(kv_prev)
        in_specs = in_specs + [pl.BlockSpec(memory_space=pl.ANY)] * 2
        aliases = {n_in: 1, n_in + 1: 2}
    return pl.pallas_call(
        _mb_prep_t_kernel,
        grid=(n_batch, n_t),
        in_specs=in_specs,
        out_specs=[q_spec, kv_spec, kv_spec],
        out_shape=[jax.ShapeDtypeStruct((n_batch, MB_HEADS, MB_HEAD_DIM, t), _F32), kv_shape, kv_shape],
        input_output_aliases=aliases,
        compiler_params=_cparams(("parallel", "parallel")),
        name="mb_prep_t",
    )(z, z, z, cos_t, sin_t, jnp.tile(qn, MB_HEADS).reshape(1, MB_WIDTH),
      jnp.tile(kn, MB_HEADS).reshape(1, MB_WIDTH), seg, *extra)


def _top3_select(gate, axis):
    n = gate.shape[axis]
    idx = lax.broadcasted_iota(jnp.int32, gate.shape, axis).astype(_F32)
    sel = jnp.zeros(gate.shape, _F32)
    picks = []
    g = gate
    for _ in range(MB_TOPK):
        m = jnp.max(g, axis=axis, keepdims=True)
        first = jnp.min(jnp.where(g == m, idx, float(n)), axis=axis, keepdims=True)
        pick = idx == first
        sel = jnp.where(pick, 1.0, sel)
        picks.append(first)
        g = jnp.where(pick, BELOW_NEG, g)
    return sel, picks


AUG_DEPTH = 2 * MB_HEAD_DIM
V_AUG = LANES
LOG2_E = 1.4426950408889634


def _mb_attn_kernel(qt_ref, kt_ref, vt_ref, o_ref, kt_aug_scr, v_aug_scr, km_scr, s_scr, *, n_blk):
    i = pl.program_id(2)
    heads = range(ATTN_HEADS)
    filler = n_blk
    ext_rows = AUG_DEPTH - MB_HEAD_DIM

    @pl.when(i == 0)
    def _():
        lane = lax.broadcasted_iota(jnp.int32, (MB_HEAD_DIM, LANES), 1)
        ext_row = lax.broadcasted_iota(jnp.int32, (ext_rows, MB_BLOCK), 0)
        ones_row = jnp.where(lax.broadcasted_iota(jnp.int32, (V_AUG - MB_HEAD_DIM, MB_BLOCK), 0) == 0, 1.0, 0.0)
        for h in heads:
            km_t = jnp.zeros((MB_HEAD_DIM, LANES), _F32)
            for j in range(n_blk):
                kt_blk = kt_ref[0, 0, h, :, j * MB_BLOCK:(j + 1) * MB_BLOCK]
                vt_blk = vt_ref[0, 0, h, :, j * MB_BLOCK:(j + 1) * MB_BLOCK]
                kt_aug_scr[h, j] = jnp.concatenate([kt_blk, jnp.where(ext_row == j, 1.0, 0.0)], axis=0).astype(_BF)
                v_aug_scr[h, j] = jnp.concatenate([vt_blk, ones_row], axis=0).T.astype(_BF)
                mean = jnp.sum(kt_blk, axis=1, keepdims=True) * (1.0 / MB_BLOCK)
                km_t = jnp.where(lane == j, mean, km_t)
            km_scr[h] = km_t.T[:n_blk]
            kt_aug_scr[h, filler] = jnp.concatenate(
                [jnp.zeros((MB_HEAD_DIM, MB_BLOCK), _F32), jnp.where(ext_row == filler, 1.0, 0.0)],
                axis=0).astype(_BF)
            v_aug_scr[h, filler] = jnp.zeros((MB_BLOCK, V_AUG), _BF)

    blk_row = lax.broadcasted_iota(jnp.int32, (n_blk, MB_BLOCK), 0)
    row8 = lax.broadcasted_iota(jnp.int32, (SUBLANES, MB_BLOCK), 0)
    q_aug = []
    for h in heads:
        q_t = qt_ref[0, h]
        gate = _dot(km_scr[h], q_t, precision=_HIGHEST)
        eligible = blk_row < i
        sel, _ = _top3_select(jnp.where(eligible, gate, NEG), 0)
        sel = jnp.where(jnp.logical_or(blk_row == i, eligible), jnp.where(blk_row == i, 1.0, sel), 0.0)
        bias = (sel - 1.0) * (-NEG)
        q_s = q_t * (MB_HEAD_DIM ** -0.5 * LOG2_E)
        tail = jnp.where(row8 == 0, NEG, 0.0)
        pad = jnp.zeros((ext_rows - n_blk - SUBLANES, MB_BLOCK), _F32)
        q_aug_t = jnp.concatenate([q_s, bias, tail, pad], axis=0)
        q_aug.append(q_aug_t.T.astype(_BF))

    n_pairs = (i + ATTN_BLOCKS_PER_TRIP - 1) // ATTN_BLOCKS_PER_TRIP

    def pair_blocks(jj):
        j0 = ATTN_BLOCKS_PER_TRIP * jj
        return (j0,) + tuple(jnp.where(j0 + r < i, j0 + r, filler) for r in range(1, ATTN_BLOCKS_PER_TRIP))

    def fold_max(s):
        return jnp.maximum(s[:, :LANES], s[:, LANES:])

    query_ge_key = (lax.broadcasted_iota(jnp.int32, (MB_BLOCK, MB_BLOCK), 0)
                    >= lax.broadcasted_iota(jnp.int32, (MB_BLOCK, MB_BLOCK), 1))
    m_init = []
    for h in heads:
        s = jnp.where(query_ge_key, _dot(q_aug[h], kt_aug_scr[h, i]), NEG)
        s_scr[h, i] = s
        m_init.append(fold_max(s))

    def score_body(jj, m):
        m = list(m)
        for j in pair_blocks(jj):
            for h in heads:
                s = _dot(q_aug[h], kt_aug_scr[h, j])
                s_scr[h, j] = s
                m[h] = jnp.maximum(m[h], fold_max(s))
        return tuple(m)

    m = lax.fori_loop(0, n_pairs, score_body, tuple(m_init))
    m_b = [jnp.broadcast_to(jnp.max(m[h], axis=1, keepdims=True), (MB_BLOCK, LANES)) for h in heads]

    def weighted_values(h, j):
        s = s_scr[h, j]
        p = jnp.concatenate([jnp.exp2(s[:, :LANES] - m_b[h]), jnp.exp2(s[:, LANES:] - m_b[h])], axis=1)
        return _dot(p.astype(_BF), v_aug_scr[h, j])

    def acc_body(jj, acc):
        acc = list(acc)
        for j in pair_blocks(jj):
            for h in heads:
                acc[h] = acc[h] + weighted_values(h, j)
        return tuple(acc)

    acc = lax.fori_loop(0, n_pairs, acc_body, tuple(weighted_values(h, i) for h in heads))
    o_ref[...] = jnp.concatenate(
        [acc[h][:, :MB_HEAD_DIM] / acc[h][:, MB_HEAD_DIM:MB_HEAD_DIM + 1] for h in heads], axis=1)


def _mb_attn(q_t, k_t_all, v_t_all, layer, n_batch, t):
    n_blk = t // MB_BLOCK
    assert n_blk % SUBLANES == 0 and MB_HEAD_DIM + n_blk + SUBLANES <= AUG_DEPTH
    n_hp = MB_HEADS // ATTN_HEADS
    kern = functools.partial(_mb_attn_kernel, n_blk=n_blk)
    kv_spec = pl.BlockSpec((1, 1, ATTN_HEADS, MB_HEAD_DIM, t), lambda b, hp, i: (b, layer, hp, 0, 0))
    return pl.pallas_call(
        kern,
        grid=(n_batch, n_hp, n_blk),
        in_specs=[pl.BlockSpec((1, ATTN_HEADS, MB_HEAD_DIM, MB_BLOCK), lambda b, hp, i: (b, hp, 0, i)),
                  kv_spec, kv_spec],
        out_specs=pl.BlockSpec((MB_BLOCK, ATTN_HEADS * MB_HEAD_DIM), lambda b, hp, i: (b * n_blk + i, hp)),
        out_shape=jax.ShapeDtypeStruct((n_batch * t, MB_WIDTH), _F32),
        scratch_shapes=[pltpu.VMEM((ATTN_HEADS, n_blk + 1, AUG_DEPTH, MB_BLOCK), _BF),
                        pltpu.VMEM((ATTN_HEADS, n_blk + 1, MB_BLOCK, V_AUG), _BF),
                        pltpu.VMEM((ATTN_HEADS, n_blk, MB_HEAD_DIM), _F32),
                        pltpu.VMEM((ATTN_HEADS, n_blk + 1, MB_BLOCK, MB_BLOCK), _F32)],
        compiler_params=_cparams(("parallel", "parallel", "arbitrary")),
        name="mb_attn",
    )(q_t, k_t_all, v_t_all)


def _kmean_kernel(pt_ref, *refs):
    page_refs, out_ref = refs[:KMEAN_PAGES_PER_STEP], refs[KMEAN_PAGES_PER_STEP]
    for jb in range(KMEAN_PAGES_PER_STEP // PAGES_PER_BLOCK):
        tot = None
        for p in range(PAGES_PER_BLOCK):
            page = page_refs[jb * PAGES_PER_BLOCK + p][0, 0]
            tot = page if tot is None else tot + page
        out_ref[0, 0, 0, :, :, jb:jb + 1] = jnp.sum(tot, axis=-1, keepdims=True) * (1.0 / MB_BLOCK)


def _kmean(cache_kt, pt_flat, n_layers, n_batch, n_pages):
    n_steps = n_pages // KMEAN_PAGES_PER_STEP
    blocks_per_step = KMEAN_PAGES_PER_STEP // PAGES_PER_BLOCK

    def page_spec(p):
        return pl.BlockSpec((1, 1, MB_HEADS, MB_HEAD_DIM, PAGE_SIZE),
                            lambda l, b, c, pt, p=p: (pt[b * n_pages + c * KMEAN_PAGES_PER_STEP + p], l, 0, 0, 0))

    grid_spec = pltpu.PrefetchScalarGridSpec(
        num_scalar_prefetch=1,
        grid=(n_layers, n_batch, n_steps),
        in_specs=[page_spec(p) for p in range(KMEAN_PAGES_PER_STEP)],
        out_specs=pl.BlockSpec((1, 1, 1, MB_HEADS, MB_HEAD_DIM, blocks_per_step),
                               lambda l, b, c, pt: (l, b, c, 0, 0, 0)),
    )
    return pl.pallas_call(
        _kmean_kernel,
        grid_spec=grid_spec,
        out_shape=jax.ShapeDtypeStruct((n_layers, n_batch, n_steps, MB_HEADS, MB_HEAD_DIM, blocks_per_step), _F32),
        compiler_params=_cparams(("parallel", "parallel", "parallel")),
        name="kmean",
    )(pt_flat, *([cache_kt] * KMEAN_PAGES_PER_STEP))


GATE_PAGE_LANE0 = SUBLANES


def _gate_kernel(q_ref, km_ref, pt_ref, idx_ref, *, own):
    n_past = km_ref.shape[3]
    n_pages = pt_ref.shape[2]
    col = lax.broadcasted_iota(jnp.int32, (SAMPLE_T_PAD, n_past), 1)
    lane = lax.broadcasted_iota(jnp.int32, (SAMPLE_T_PAD, LANES), 1)
    page_col = lax.broadcasted_iota(jnp.int32, (SAMPLE_T_PAD, n_pages), 1).astype(_F32)
    pt_row = pt_ref[0].astype(_F32)
    for h in range(MB_HEADS):
        gate = _dot(q_ref[0, h], km_ref[0, h], precision=_HIGHEST)
        _, picks = _top3_select(jnp.where(col < own, gate, NEG), 1)
        out = jnp.zeros((SAMPLE_T_PAD, LANES), jnp.int32)
        for r, first in enumerate(picks):
            out = jnp.where(lane == r, first.astype(jnp.int32), out)
            for p in range(PAGES_PER_BLOCK):
                logical = first * PAGES_PER_BLOCK + p
                phys = jnp.sum(jnp.where(page_col == logical, pt_row, 0.0), axis=1, keepdims=True)
                out = jnp.where(lane == GATE_PAGE_LANE0 + r * PAGES_PER_BLOCK + p, phys.astype(jnp.int32), out)
        idx_ref[0, h] = out


def _gate(q_s, kmean_l, page_table, own):
    n_batch = q_s.shape[0]
    n_past = kmean_l.shape[3]
    n_pages = page_table.shape[1]
    return pl.pallas_call(
        functools.partial(_gate_kernel, own=own),
        grid=(n_batch,),
        in_specs=[pl.BlockSpec((1, MB_HEADS, SAMPLE_T_PAD, MB_HEAD_DIM), lambda b: (b, 0, 0, 0)),
                  pl.BlockSpec((1, MB_HEADS, MB_HEAD_DIM, n_past), lambda b: (b, 0, 0, 0)),
                  pl.BlockSpec((1, 1, n_pages), lambda b: (b, 0, 0))],
        out_specs=pl.BlockSpec((1, MB_HEADS, SAMPLE_T_PAD, LANES), lambda b: (b, 0, 0, 0)),
        out_shape=jax.ShapeDtypeStruct((n_batch, MB_HEADS, SAMPLE_T_PAD, LANES), jnp.int32),
        compiler_params=_cparams(("parallel",)),
        name="mb_gate",
    )(q_s, kmean_l, page_table.reshape(n_batch, 1, n_pages))


def _sample_attn_kernel(pg_ref, idx_ref, q_ref, kn_ref, vn_ref, ck_ref, cv_ref, o_ref, kbuf, vbuf, sem,
                        *, t_s, own, layer):
    n_slots = t_s * MB_TOPK * PAGES_PER_BLOCK
    b = pl.program_id(0)
    row = lax.broadcasted_iota(jnp.int32, (SAMPLE_T_PAD, 1), 0)

    def page_copies(h, par):
        base = (b * MB_HEADS + h) * n_slots
        out = []
        for n in range(n_slots):
            page = pg_ref[base + n]
            out.append(pltpu.make_async_copy(ck_ref.at[page, layer, h], kbuf.at[par, n], sem.at[0, par]))
            out.append(pltpu.make_async_copy(cv_ref.at[page, layer, h], vbuf.at[par, n], sem.at[1, par]))
        return out

    for cp in page_copies(0, 0):
        cp.start()

    def head_body(h, carry):
        par = h % 2

        @pl.when(h + 1 < MB_HEADS)
        def _():
            for cp in page_copies(h + 1, 1 - par):
                cp.start()

        for cp in page_copies(h, par):
            cp.wait()

        q = q_ref[0, h] * (MB_HEAD_DIM ** -0.5)
        q_bf = q.astype(_BF)
        scores = []
        for t in range(t_s):
            for j in range(MB_TOPK):
                blk = idx_ref[((b * MB_HEADS + h) * t_s + t) * MB_TOPK + j]
                allowed = row == jnp.where(blk < own, t, -1)
                for p in range(PAGES_PER_BLOCK):
                    slot = (t * MB_TOPK + j) * PAGES_PER_BLOCK + p
                    s = _dot(q_bf, kbuf[par, slot].astype(_BF))
                    scores.append(jnp.where(allowed, s, NEG))
        kn = kn_ref[0, h]
        vn = vn_ref[0, h]
        own_scores = []
        for c in range(t_s):
            s = jnp.sum(q * kn[c:c + 1, :], axis=1, keepdims=True)
            own_scores.append(jnp.where(row >= c, s, NEG))

        m = own_scores[0]
        for s in own_scores[1:]:
            m = jnp.maximum(m, s)
        for s in scores:
            m = jnp.maximum(m, jnp.max(s, axis=1, keepdims=True))
        l = jnp.zeros((SAMPLE_T_PAD, 1), _F32)
        acc = jnp.zeros((SAMPLE_T_PAD, MB_HEAD_DIM), _F32)
        for c, s in enumerate(own_scores):
            p = jnp.exp(s - m)
            l = l + p
            acc = acc + p * vn[c:c + 1, :]
        for slot, s in enumerate(scores):
            p = jnp.exp(s - m)
            l = l + jnp.sum(p, axis=1, keepdims=True)
            acc = acc + _dot_nt(p.astype(_BF), vbuf[par, slot].astype(_BF))
        o_ref[0, h] = acc / l
        return carry

    lax.fori_loop(0, MB_HEADS, head_body, 0)


def _sample_attn(q_s, k_s, v_s, cache_kt, cache_vt, pages_flat, idx_flat, layer, t_s, own):
    n_batch = q_s.shape[0]
    n_slots = t_s * MB_TOPK * PAGES_PER_BLOCK
    new_spec = pl.BlockSpec((1, MB_HEADS, SAMPLE_T_PAD, MB_HEAD_DIM), lambda b, pg, idx: (b, 0, 0, 0))
    hbm = pl.BlockSpec(memory_space=pl.ANY)
    grid_spec = pltpu.PrefetchScalarGridSpec(
        num_scalar_prefetch=2,
        grid=(n_batch,),
        in_specs=[new_spec, new_spec, new_spec, hbm, hbm],
        out_specs=new_spec,
        scratch_shapes=[pltpu.VMEM((2, n_slots, MB_HEAD_DIM, PAGE_SIZE), _F32),
                        pltpu.VMEM((2, n_slots, MB_HEAD_DIM, PAGE_SIZE), _F32),
                        pltpu.SemaphoreType.DMA((2, 2))],
    )
    return pl.pallas_call(
        functools.partial(_sample_attn_kernel, t_s=t_s, own=own, layer=layer),
        grid_spec=grid_spec,
        out_shape=jax.ShapeDtypeStruct((n_batch, MB_HEADS, SAMPLE_T_PAD, MB_HEAD_DIM), _F32),
        compiler_params=_cparams(("parallel",)),
        name="mb_sample_attn",
    )(pages_flat, idx_flat, q_s, k_s, v_s, cache_kt, cache_vt)


def _pool_kernel(u_ref, hist_ref, w_ref, sc_ref, y_ref, hist_out_ref, ext_scr, *, tm, pos0, t_last):
    ti = pl.program_id(1)

    @pl.when(ti == 0)
    def _():
        ext_scr[0:POOL_HALO, :] = hist_ref[0]

    u = u_ref[...].astype(_F32)
    ext_scr[POOL_HALO:POOL_HALO + tm, :] = u
    pos = pos0 + ti * tm + lax.broadcasted_iota(jnp.int32, (tm, 1), 0)
    for g, w in enumerate(POOL_WINDOWS):
        sl = slice(g * POOL_GROUP_DIM, (g + 1) * POOL_GROUP_DIM)
        tot = u[:, sl]
        for d in range(1, w):
            tot = tot + ext_scr[POOL_HALO - d:POOL_HALO - d + tm, sl]
        cnt = jnp.minimum(pos + 1, w).astype(_F32)
        diff = tot / cnt - u[:, sl]
        y = _dot(diff.astype(_BF), w_ref[g])
        y_ref[:, sl] = y * sc_ref[:, sl]

    @pl.when(ti == pl.num_programs(1) - 1)
    def _():
        hist_out_ref[0] = ext_scr[t_last:t_last + POOL_HALO, :]

    @pl.when(ti < pl.num_programs(1) - 1)
    def _():
        ext_scr[0:POOL_HALO, :] = ext_scr[tm:tm + POOL_HALO, :]


def _pool(z, hist16, pool_w_bf, pool_scale, n_batch, t_pad, tm, pos0, t_valid):
    n_t = t_pad // tm
    t_last = t_valid - (n_t - 1) * tm
    hist_spec = pl.BlockSpec((1, POOL_HALO, POOL_WIDTH), lambda b, ti: (b, 0, 0))
    kern = functools.partial(_pool_kernel, tm=tm, pos0=pos0, t_last=t_last)
    return pl.pallas_call(
        kern,
        grid=(n_batch, n_t),
        in_specs=[pl.BlockSpec((tm, POOL_WIDTH), lambda b, ti: (b * n_t + ti, COL_UC)),
                  hist_spec,
                  pl.BlockSpec((POOL_GROUPS, POOL_GROUP_DIM, POOL_GROUP_DIM), lambda b, ti: (0, 0, 0)),
                  pl.BlockSpec((1, POOL_WIDTH), lambda b, ti: (0, 0))],
        out_specs=[pl.BlockSpec((tm, POOL_WIDTH), lambda b, ti: (b * n_t + ti, 0)), hist_spec],
        out_shape=[jax.ShapeDtypeStruct((n_batch * t_pad, POOL_WIDTH), _F32),
                   jax.ShapeDtypeStruct((n_batch, POOL_HALO, POOL_WIDTH), _F32)],
        scratch_shapes=[pltpu.VMEM((POOL_HALO + tm, POOL_WIDTH), _F32)],
        compiler_params=_cparams(("parallel", "arbitrary")),
        name="pool",
    )(z, hist16, pool_w_bf, pool_scale.reshape(1, POOL_WIDTH))


def _mix_kernel(x_ref, a_ref, b_ref, c_ref, g0_ref, g1_ref, g2_ref, wb_ref, wo_ref, gn_ref, w1_ref, w2_ref, o_ref):
    merged = None
    for br_ref, g_ref, n in ((a_ref, g0_ref, 0), (b_ref, g1_ref, 1), (c_ref, g2_ref, 2)):
        proj = _dot(br_ref[...].astype(_BF), wb_ref[n])
        term = _sigmoid(g_ref[...].astype(_F32)) * proj
        merged = term if merged is None else merged + term
    x = x_ref[...] + _dot(merged.astype(_BF), wo_ref[...])
    ms = jnp.mean(x * x, axis=-1, keepdims=True)
    h = (x * lax.rsqrt(ms + EPS) * gn_ref[...]).astype(_BF)
    hid = jnp.maximum(_dot(h, w1_ref[...]), 0.0)
    hid = (hid * hid).astype(_BF)
    o_ref[...] = x + _dot(hid, w2_ref[...])


def _mix(x, a, b, c, z, wb_bf, wo_bf, g_ffn, w1_bf, w2_bf, tm):
    n = x.shape[0]
    row = lambda w: pl.BlockSpec((tm, w), lambda i: (i, 0))
    gate = lambda k: pl.BlockSpec((tm, D_MODEL), lambda i, k=k: (i, COL_GATE_1024 + k))
    const = lambda shape: pl.BlockSpec(shape, lambda i: (0,) * len(shape), pipeline_mode=pl.Buffered(1))
    return pl.pallas_call(
        _mix_kernel,
        grid=(n // tm,),
        in_specs=[row(D_MODEL), row(BRANCH_WIDTH), row(BRANCH_WIDTH), row(BRANCH_WIDTH),
                  gate(0), gate(1), gate(2),
                  const((N_BRANCH, BRANCH_WIDTH, D_MODEL)), const((D_MODEL, D_MODEL)),
                  const((1, D_MODEL)), const((D_MODEL, D_FF)), const((D_FF, D_MODEL))],
        out_specs=row(D_MODEL),
        out_shape=jax.ShapeDtypeStruct((n, D_MODEL), _F32),
        compiler_params=_cparams(("parallel",)),
        name="mix",
    )(x, a, b, c, z, z, z, wb_bf, wo_bf, g_ffn.reshape(1, D_MODEL), w1_bf, w2_bf)


def _rope_tables(pos):
    half = MB_HEAD_DIM // 2
    inv = ROPE_THETA ** (-jnp.arange(half, dtype=_F32) / half)
    ang = pos.astype(_F32)[:, None] * inv[None, :]
    cos, sin = jnp.cos(ang), jnp.sin(ang)
    cos_t = jnp.concatenate([cos, cos, cos, cos], axis=1)
    sin_t = jnp.concatenate([-sin, sin, -sin, sin], axis=1)
    return cos_t, sin_t


def _tile_rows(n, cap):
    t = cap
    while n % t:
        t //= 2
    return t


def _layer(x, *, n_batch, t_pad, t_valid, pos0, chunk, sub, lb, s0_t, hist16, moba, w):
    n = x.shape[0]
    z_dtype = _BF if t_pad % (2 * SUBLANES) == 0 else _F32
    z = _inproj(x, w["norm_mix"], w["w_in"], _tile_rows(n, 1024), D_IN // 4, z_dtype)
    a_out, s_new_t = _hgrn(z, lb, w["hg_onorm"], s0_t, n_batch, t_pad, chunk, sub, t_valid)
    b_out, kv = moba(z)
    c_out, hist_new = _pool(z, hist16, w["pool_w"], w["pool_scale"], n_batch, t_pad, _tile_rows(t_pad, 512),
                            pos0, t_valid)
    x2 = _mix(x, a_out, b_out, c_out, z, w["w_branch"], w["w_out"], w["norm_ffn"], w["w_ff1"], w["w_ff2"],
              _tile_rows(n, 256))
    return x2, kv, s_new_t, hist_new


def kernel(x_prompt, x_sample, cache_k, cache_v, state_hgrn, state_pool, page_table, norm_mix, w_in, hg_lb,
           hg_onorm, mb_qnorm, mb_knorm, pool_w, pool_scale, w_branch, w_out, norm_ffn, w_ff1, w_ff2):
    n_b, t_p, _ = x_prompt.shape
    n_db, t_s, _ = x_sample.shape
    depth = w_in.shape[0]
    n_pages = page_table.shape[1]
    past_len = n_pages * PAGE_SIZE
    assert t_p % MB_BLOCK == 0 and t_p % HG_CHUNK == 0
    assert past_len % MB_BLOCK == 0 and t_s <= SAMPLE_T_PAD and n_pages % KMEAN_PAGES_PER_STEP == 0
    own_s = past_len // MB_BLOCK

    lb_soft = jax.nn.softmax(hg_lb.astype(_F32), axis=0)
    lb_all = jnp.cumsum(lb_soft, axis=0) - lb_soft[0:1]

    seg = (jnp.arange(MB_WIDTH)[:, None] // MB_HEAD_DIM == jnp.arange(MB_WIDTH)[None, :] // MB_HEAD_DIM).astype(_BF)
    cos_p, sin_p = _rope_tables(jnp.arange(t_p, dtype=jnp.int32))
    cos_s, sin_s = _rope_tables(past_len + jnp.arange(SAMPLE_T_PAD, dtype=jnp.int32))

    cache_kt = jnp.swapaxes(cache_k, 3, 4)
    cache_vt = jnp.swapaxes(cache_v, 3, 4)
    pt_flat = page_table.reshape(-1).astype(jnp.int32)
    kmean_all = _kmean(cache_kt, pt_flat, depth, n_db, n_pages)
    kmean_all = jnp.transpose(kmean_all, (0, 1, 3, 4, 2, 5)).reshape(depth, n_db, MB_HEADS, MB_HEAD_DIM, own_s)

    xp = x_prompt.reshape(n_b * t_p, D_MODEL)
    xs = jnp.pad(x_sample, ((0, 0), (0, SAMPLE_T_PAD - t_s), (0, 0))).reshape(n_db * SAMPLE_T_PAD, D_MODEL)

    s0_p = jnp.zeros((n_b, HG_HEADS, HG_DV, HG_DK), _F32)
    hist_p = jnp.zeros((n_b, POOL_HALO, POOL_WIDTH), _F32)

    kv_p = None
    ks_l, vs_l, sp_l, ss_l, hp_l, hs_l = [], [], [], [], [], []
    for l in range(depth):
        w = dict(norm_mix=norm_mix[l], w_in=w_in[l].astype(_BF), hg_onorm=hg_onorm[l], mb_qnorm=mb_qnorm[l],
                 mb_knorm=mb_knorm[l], pool_w=pool_w[l].astype(_BF), pool_scale=pool_scale[l],
                 w_branch=w_branch[l].astype(_BF), w_out=w_out[l].astype(_BF), norm_ffn=norm_ffn[l],
                 w_ff1=w_ff1[l].astype(_BF), w_ff2=w_ff2[l].astype(_BF))

        def moba_p(z, l=l, w=w, kv_prev=kv_p):
            q_t, k_t_all, v_t_all = _mb_prep_t(z, cos_p, sin_p, w["mb_qnorm"], w["mb_knorm"], seg, n_b, t_p,
                                               _tile_rows(t_p, 512), l, depth, kv_prev)
            return _mb_attn(q_t, k_t_all, v_t_all, l, n_b, t_p), (k_t_all, v_t_all)

        xp, kv_p, sp_t, hp = _layer(
            xp, n_batch=n_b, t_pad=t_p, t_valid=t_p, pos0=0, chunk=HG_CHUNK, sub=HG_SUB, lb=lb_all[l],
            s0_t=s0_p, hist16=hist_p, moba=moba_p, w=w)

        def moba_s(z, l=l, w=w):
            q, k, v = _mb_prep(z, cos_s, sin_s, w["mb_qnorm"], w["mb_knorm"], seg, n_db, SAMPLE_T_PAD, SAMPLE_T_PAD)
            idx = _gate(q, kmean_all[l], page_table, own_s)
            idx_flat = idx[:, :, :t_s, :MB_TOPK].reshape(-1)
            pages_flat = idx[:, :, :t_s, GATE_PAGE_LANE0:GATE_PAGE_LANE0 + MB_TOPK * PAGES_PER_BLOCK].reshape(-1)
            o = _sample_attn(q, k, v, cache_kt, cache_vt, pages_flat, idx_flat, l, t_s, own_s)
            return jnp.transpose(o, (0, 2, 1, 3)).reshape(n_db * SAMPLE_T_PAD, MB_WIDTH), (k, v)

        s0_s = jnp.swapaxes(state_hgrn[l], -1, -2)
        hist_s = jnp.pad(state_pool[l], ((0, 0), (POOL_HALO - POOL_HIST, 0), (0, 0)))
        xs, (ks, vs), ss_t, hs = _layer(
            xs, n_batch=n_db, t_pad=SAMPLE_T_PAD, t_valid=t_s, pos0=past_len, chunk=SAMPLE_T_PAD,
            sub=SAMPLE_T_PAD, lb=lb_all[l], s0_t=s0_s, hist16=hist_s, moba=moba_s, w=w)

        ks_l.append(ks[:, :, :t_s]); vs_l.append(vs[:, :, :t_s])
        sp_l.append(jnp.swapaxes(sp_t, -1, -2)); ss_l.append(jnp.swapaxes(ss_t, -1, -2))
        hp_l.append(hp[:, POOL_HALO - POOL_HIST:]); hs_l.append(hs[:, POOL_HALO - POOL_HIST:])

    y_p = xp.reshape(n_b, t_p, D_MODEL)
    y_s = xs.reshape(n_db, SAMPLE_T_PAD, D_MODEL)[:, :t_s]
    k_p = jnp.swapaxes(kv_p[0], 3, 4)
    v_p = jnp.swapaxes(kv_p[1], 3, 4)
    return (y_p, y_s, k_p, v_p, jnp.stack(ks_l, axis=1), jnp.stack(vs_l, axis=1),
            jnp.stack(sp_l, axis=0), jnp.stack(ss_l, axis=0), jnp.stack(hp_l, axis=0), jnp.stack(hs_l, axis=0))
```

```python
import functools

import jax
import jax.numpy as jnp
from jax import lax
from jax.experimental import pallas as pl
from jax.experimental.pallas import tpu as pltpu

D_MODEL = 1024
PAGE_SIZE = 128
HG_HEADS = 4
HG_DK = 128
HG_DV = 128
HG_WIDTH = HG_HEADS * HG_DV
HG_CHUNK = 64
HG_SUB = 16
HG_SEQS_PER_STEP = 2
MB_HEADS = 8
MB_HEAD_DIM = 64
MB_WIDTH = MB_HEADS * MB_HEAD_DIM
MB_BLOCK = 256
MB_TOPK = 3
ROPE_THETA = 10000.0
POOL_WINDOWS = (2, 4, 8, 16)
POOL_GROUPS = 4
POOL_GROUP_DIM = 128
POOL_WIDTH = POOL_GROUPS * POOL_GROUP_DIM
POOL_HIST = 15
POOL_HALO = 16
N_BRANCH = 3
BRANCH_WIDTH = 512
D_FF = 4 * D_MODEL
EPS = 1e-6
NEG = -1e30
BELOW_NEG = -3e38
F_FLOOR = 1e-30
D_IN = 4 * HG_WIDTH + 3 * MB_WIDTH + POOL_WIDTH + N_BRANCH * D_MODEL
COL_QA, COL_FA, COL_IA, COL_GA, COL_QB, COL_KB, COL_VB, COL_UC = range(8)
COL_GATE_1024 = 4

LANES = 128
SUBLANES = 8
SAMPLE_T_PAD = SUBLANES
VMEM_LIMIT = 56 * 1024 * 1024
PAGES_PER_BLOCK = MB_BLOCK // PAGE_SIZE
KMEAN_PAGES_PER_STEP = 64
HEADS_PER_STEP = LANES // MB_HEAD_DIM
ATTN_HEADS = 4
ATTN_BLOCKS_PER_TRIP = 3

_BF = jnp.bfloat16
_F32 = jnp.float32
_HIGHEST = lax.Precision.HIGHEST


def _cparams(sem, vmem=VMEM_LIMIT):
    return pltpu.CompilerParams(dimension_semantics=sem, vmem_limit_bytes=vmem)


def _dot(a, b, precision=None):
    return jnp.dot(a, b, preferred_element_type=_F32, precision=precision)


def _dot_nt(a, b, precision=None):
    return lax.dot_general(a, b, (((1,), (1,)), ((), ())), preferred_element_type=_F32, precision=precision)


def _dot_tn(a, b):
    return lax.dot_general(a, b, (((0,), (0,)), ((), ())), preferred_element_type=_F32)


def _sigmoid(x):
    return 1.0 / (1.0 + jnp.exp(-x))


def _inproj_kernel(x_ref, g_ref, w_ref, z_ref, h_ref):
    @pl.when(pl.program_id(1) == 0)
    def _():
        x = x_ref[...]
        ms = jnp.mean(x * x, axis=-1, keepdims=True)
        h_ref[...] = (x * lax.rsqrt(ms + EPS) * g_ref[...]).astype(_BF)

    z_ref[...] = _dot(h_ref[...], w_ref[...]).astype(z_ref.dtype)


def _inproj(x, g, w_bf, tm, tn, z_dtype):
    n = x.shape[0]
    return pl.pallas_call(
        _inproj_kernel,
        grid=(n // tm, D_IN // tn),
        in_specs=[
            pl.BlockSpec((tm, D_MODEL), lambda i, j: (i, 0)),
            pl.BlockSpec((1, D_MODEL), lambda i, j: (0, 0)),
            pl.BlockSpec((D_MODEL, tn), lambda i, j: (0, j)),
        ],
        out_specs=pl.BlockSpec((tm, tn), lambda i, j: (i, j)),
        out_shape=jax.ShapeDtypeStruct((n, D_IN), z_dtype),
        scratch_shapes=[pltpu.VMEM((tm, D_MODEL), _BF)],
        compiler_params=_cparams(("parallel", "arbitrary")),
        name="inproj",
    )(x, g.reshape(1, D_MODEL), w_bf)


def _hgrn_kernel(q_ref, f_ref, i_ref, g_ref, lb_ref, on_ref, s0_ref, a_ref, s_out_ref,
                 st_scr, b_scr, k_scr, v_scr, *, chunk, cps, sub, t_valid):
    ci = pl.program_id(1)

    @pl.when(ci == 0)
    def _():
        for cc in range(cps):
            for h in range(HG_HEADS):
                st_scr[cc * HG_HEADS + h] = s0_ref[cc, h]

    row = lax.broadcasted_iota(jnp.int32, (chunk, 1), 0)
    row8 = lax.broadcasted_iota(jnp.int32, (SUBLANES, 1), 0)
    tril = (lax.broadcasted_iota(jnp.int32, (chunk, chunk), 0)
            >= lax.broadcasted_iota(jnp.int32, (chunk, chunk), 1)).astype(_F32)
    n_sub = chunk // sub
    heads = range(HG_HEADS)
    lanes = [slice(h * HG_DK, (h + 1) * HG_DK) for h in heads]
    units = [(h, cc) for cc in range(cps) for h in heads]
    q_u, k_u, vbf_u, b_u = {}, {}, {}, {}

    for h, cc in units:
        sl, hc = lanes[h], h * cps + cc
        qr = q_ref[cc, :, sl].astype(_F32)
        fr = f_ref[cc, :, sl].astype(_F32)
        v = i_ref[cc, :, sl].astype(_F32)
        lb = lb_ref[:, sl]
        fval = lb + (1.0 - lb) * _sigmoid(fr)
        log2f = jnp.log2(jnp.maximum(fval, F_FLOOR))
        k = (1.0 - lb) * _sigmoid(-fr)
        if t_valid < chunk:
            valid = row < t_valid
            log2f = jnp.where(valid, log2f, 0.0)
            k = jnp.where(valid, k, 0.0)
        b = _dot(tril, log2f, precision=_HIGHEST)
        b_scr[hc] = b
        k_scr[hc] = k
        v_scr[hc] = v
        q_u[h, cc], k_u[h, cc], vbf_u[h, cc], b_u[h, cc] = qr * _sigmoid(qr), k, v.astype(_BF), b

    st = {(h, cc): st_scr[cc * HG_HEADS + h] for h, cc in units}
    acc = {}
    for cc in range(cps):
        for h in heads:
            q, k, v_bf, b = q_u[h, cc], k_u[h, cc], vbf_u[h, cc], b_u[h, cc]
            o = _dot_nt((q * jnp.exp2(b)).astype(_BF), st[h, cc].astype(_BF))
            b_last = b[chunk - 1:chunk]
            k_end = (k * jnp.exp2(b_last - b)).astype(_BF)
            st[h, cc] = st[h, cc] * jnp.exp2(b_last) + _dot_tn(v_bf, k_end)
            for si in range(n_sub):
                r0 = si * sub
                o_s = o[r0:r0 + sub]
                if si > 0:
                    ref = b[r0 - 1:r0]
                    q_dec = (q[r0:r0 + sub] * jnp.exp2(b[r0:r0 + sub] - ref)).astype(_BF)
                    k_dec = (k[:r0] * jnp.exp2(ref - b[:r0])).astype(_BF)
                    att = _dot_nt(q_dec, k_dec)
                    o_s = o_s + _dot(att.astype(_BF), v_bf[:r0])
                for g0 in range(0, sub, SUBLANES):
                    acc[h, cc, r0 + g0] = o_s[g0:g0 + SUBLANES]
    for h, cc in units:
        st_scr[cc * HG_HEADS + h] = st[h, cc]

    for si in range(n_sub):
        r0 = si * sub
        for g0 in range(0, sub, SUBLANES):
            rg = slice(r0 + g0, r0 + g0 + SUBLANES)
            for s in range(g0 + SUBLANES):
                rs = slice(r0 + s, r0 + s + 1)
                for h, cc in units:
                    hc = h * cps + cc
                    d = b_u[h, cc][rg] - b_scr[hc, rs, :]
                    if s > g0:
                        d = jnp.where(row8 >= s - g0, d, NEG)
                    p = q_u[h, cc][rg] * jnp.exp2(d) * k_scr[hc, rs, :]
                    acc[h, cc, r0 + g0] = acc[h, cc, r0 + g0] + jnp.sum(p, axis=-1, keepdims=True) * v_scr[hc, rs, :]

    for h, cc in units:
        sl = lanes[h]
        o = jnp.concatenate([acc[h, cc, r] for r in range(0, chunk, SUBLANES)], axis=0) if chunk > SUBLANES \
            else acc[h, cc, 0]
        gr = g_ref[cc, :, sl].astype(_F32)
        ms = jnp.mean(o * o, axis=-1, keepdims=True)
        a_ref[cc, :, sl] = o * lax.rsqrt(ms + EPS) * on_ref[...] * (gr * _sigmoid(gr))

    @pl.when(ci == pl.num_programs(1) - 1)
    def _():
        for cc in range(cps):
            for h in range(HG_HEADS):
                s_out_ref[cc, h] = st_scr[cc * HG_HEADS + h]


def _hgrn(z, lb, onorm, s0_t, n_batch, t_pad, chunk, sub, t_valid):
    cps = HG_SEQS_PER_STEP
    assert n_batch % cps == 0 and t_pad % chunk == 0
    n_chunks = t_pad // chunk
    z3 = z.reshape(n_batch, t_pad, D_IN)
    col = lambda c: pl.BlockSpec((cps, chunk, HG_WIDTH), lambda b, ci, c=c: (b, ci, c))
    state_spec = pl.BlockSpec((cps, HG_HEADS, HG_DV, HG_DK), lambda b, ci: (b, 0, 0, 0))
    kern = functools.partial(_hgrn_kernel, chunk=chunk, cps=cps, sub=sub, t_valid=t_valid)
    a_out, s_new = pl.pallas_call(
        kern,
        grid=(n_batch // cps, n_chunks),
        in_specs=[col(COL_QA), col(COL_FA), col(COL_IA), col(COL_GA),
                  pl.BlockSpec((1, HG_WIDTH), lambda b, ci: (0, 0)),
                  pl.BlockSpec((1, HG_DV), lambda b, ci: (0, 0)),
                  state_spec],
        out_specs=[pl.BlockSpec((cps, chunk, HG_WIDTH), lambda b, ci: (b, ci, 0)), state_spec],
        out_shape=[jax.ShapeDtypeStruct((n_batch, t_pad, HG_WIDTH), _F32),
                   jax.ShapeDtypeStruct((n_batch, HG_HEADS, HG_DV, HG_DK), _F32)],
        scratch_shapes=[pltpu.VMEM((HG_HEADS * cps, HG_DV, HG_DK), _F32),
                        pltpu.VMEM((HG_HEADS * cps, chunk, HG_DK), _F32),
                        pltpu.VMEM((HG_HEADS * cps, chunk, HG_DK), _F32),
                        pltpu.VMEM((HG_HEADS * cps, chunk, HG_DV), _F32)],
        compiler_params=_cparams(("parallel", "arbitrary")),
        name="hgrn",
    )(z3, z3, z3, z3, lb.reshape(1, HG_WIDTH), onorm.reshape(1, HG_DV), s0_t)
    return a_out.reshape(n_batch * t_pad, HG_WIDTH), s_new


def _norm_rope(x, w, cos, sin, seg):
    lane = lax.broadcasted_iota(jnp.int32, (1, LANES), 1)
    first_half = (lane % MB_HEAD_DIM) < (MB_HEAD_DIM // 2)
    sq = x * x
    hi = sq.astype(_BF)
    lo = (sq - hi.astype(_F32)).astype(_BF)
    ss = _dot(hi, seg) + _dot(lo, seg)
    xn = x * lax.rsqrt(ss * (1.0 / MB_HEAD_DIM) + EPS) * w
    outs = []
    for g in range(MB_WIDTH // LANES):
        xg = xn[:, g * LANES:(g + 1) * LANES]
        rot = jnp.where(first_half, pltpu.roll(xg, LANES - MB_HEAD_DIM // 2, 1),
                        pltpu.roll(xg, MB_HEAD_DIM // 2, 1))
        outs.append(xg * cos + rot * sin)
    return outs


def _mb_prep_kernel(q_ref, k_ref, v_ref, cos_ref, sin_ref, qn_ref, kn_ref, seg_ref,
                    qo_ref, ko_ref, vo_ref):
    q_g = _norm_rope(q_ref[...].astype(_F32), qn_ref[...], cos_ref[...], sin_ref[...], seg_ref[...])
    k_g = _norm_rope(k_ref[...].astype(_F32), kn_ref[...], cos_ref[...], sin_ref[...], seg_ref[...])
    v = v_ref[...].astype(_F32)
    for h in range(MB_HEADS):
        g, off = divmod(h * MB_HEAD_DIM, LANES)
        qo_ref[0, h] = q_g[g][:, off:off + MB_HEAD_DIM]
        ko_ref[0, h] = k_g[g][:, off:off + MB_HEAD_DIM]
        vo_ref[0, h] = v[:, h * MB_HEAD_DIM:(h + 1) * MB_HEAD_DIM]


def _mb_prep_t_kernel(q_ref, k_ref, v_ref, cos_ref, sin_ref, qn_ref, kn_ref, seg_ref, *rest):
    qo_ref, ko_ref, vo_ref = rest[-3:]
    q_g = _norm_rope(q_ref[...].astype(_F32), qn_ref[...], cos_ref[...], sin_ref[...], seg_ref[...])
    k_g = _norm_rope(k_ref[...].astype(_F32), kn_ref[...], cos_ref[...], sin_ref[...], seg_ref[...])
    v = v_ref[...].astype(_F32)
    for g in range(MB_WIDTH // LANES):
        q_t = q_g[g].T
        k_t = k_g[g].T
        v_t = v[:, g * LANES:(g + 1) * LANES].T
        for hh in range(HEADS_PER_STEP):
            h = g * HEADS_PER_STEP + hh
            rows = slice(hh * MB_HEAD_DIM, (hh + 1) * MB_HEAD_DIM)
            qo_ref[0, h] = q_t[rows]
            ko_ref[0, 0, h] = k_t[rows]
            vo_ref[0, 0, h] = v_t[rows]


def _prep_common_specs(n_t, tm):
    col = lambda c: pl.BlockSpec((tm, MB_WIDTH), lambda b, ti, c=c: (b * n_t + ti, c))
    tab = pl.BlockSpec((tm, LANES), lambda b, ti: (ti, 0))
    vec = pl.BlockSpec((1, MB_WIDTH), lambda b, ti: (0, 0))
    return [col(COL_QB), col(COL_KB), col(COL_VB), tab, tab, vec, vec,
            pl.BlockSpec((MB_WIDTH, MB_WIDTH), lambda b, ti: (0, 0))]


def _mb_prep(z, cos_t, sin_t, qn, kn, seg, n_batch, t_pad, tm):
    n_t = t_pad // tm
    out = pl.BlockSpec((1, MB_HEADS, tm, MB_HEAD_DIM), lambda b, ti: (b, 0, ti, 0))
    shp = jax.ShapeDtypeStruct((n_batch, MB_HEADS, t_pad, MB_HEAD_DIM), _F32)
    return pl.pallas_call(
        _mb_prep_kernel,
        grid=(n_batch, n_t),
        in_specs=_prep_common_specs(n_t, tm),
        out_specs=[out, out, out],
        out_shape=[shp, shp, shp],
        compiler_params=_cparams(("parallel", "parallel")),
        name="mb_prep",
    )(z, z, z, cos_t, sin_t, jnp.tile(qn, MB_HEADS).reshape(1, MB_WIDTH),
      jnp.tile(kn, MB_HEADS).reshape(1, MB_WIDTH), seg)


def _mb_prep_t(z, cos_t, sin_t, qn, kn, seg, n_batch, t, tm, layer, depth, kv_prev):
    n_t = t // tm
    q_spec = pl.BlockSpec((1, MB_HEADS, MB_HEAD_DIM, tm), lambda b, ti: (b, 0, 0, ti))
    kv_spec = pl.BlockSpec((1, 1, MB_HEADS, MB_HEAD_DIM, tm), lambda b, ti: (b, layer, 0, 0, ti))
    kv_shape = jax.ShapeDtypeStruct((n_batch, depth, MB_HEADS, MB_HEAD_DIM, t), _F32)
    in_specs = _prep_common_specs(n_t, tm)
    n_in = len(in_specs)
    extra, aliases = (), {}
    if kv_prev is not None:
        extra = tuple(kv_prev)
        in_specs = in_specs + [pl.BlockSpec(memory_space=pl.ANY)] * 2
        aliases = {n_in: 1, n_in + 1: 2}
    return pl.pallas_call(
        _mb_prep_t_kernel,
        grid=(n_batch, n_t),
        in_specs=in_specs,
        out_specs=[q_spec, kv_spec, kv_spec],
        out_shape=[jax.ShapeDtypeStruct((n_batch, MB_HEADS, MB_HEAD_DIM, t), _F32), kv_shape, kv_shape],
        input_output_aliases=aliases,
        compiler_params=_cparams(("parallel", "parallel")),
        name="mb_prep_t",
    )(z, z, z, cos_t, sin_t, jnp.tile(qn, MB_HEADS).reshape(1, MB_WIDTH),
      jnp.tile(kn, MB_HEADS).reshape(1, MB_WIDTH), seg, *extra)


def _top3_select(gate, axis):
    n = gate.shape[axis]
    idx = lax.broadcasted_iota(jnp.int32, gate.shape, axis).astype(_F32)
    sel = jnp.zeros(gate.shape, _F32)
    picks = []
    g = gate
    for _ in range(MB_TOPK):
        m = jnp.max(g, axis=axis, keepdims=True)
        first = jnp.min(jnp.where(g == m, idx, float(n)), axis=axis, keepdims=True)
        pick = idx == first
        sel = jnp.where(pick, 1.0, sel)
        picks.append(first)
        g = jnp.where(pick, BELOW_NEG, g)
    return sel, picks


AUG_DEPTH = 2 * MB_HEAD_DIM
V_AUG = LANES
LOG2_E = 1.4426950408889634


def _mb_attn_kernel(qt_ref, kt_ref, vt_ref, o_ref, kt_aug_scr, v_aug_scr, km_scr, s_scr, *, n_blk):
    i = pl.program_id(2)
    heads = range(ATTN_HEADS)
    filler = n_blk
    ext_rows = AUG_DEPTH - MB_HEAD_DIM

    @pl.when(i == 0)
    def _():
        lane = lax.broadcasted_iota(jnp.int32, (MB_HEAD_DIM, LANES), 1)
        ext_row = lax.broadcasted_iota(jnp.int32, (ext_rows, MB_BLOCK), 0)
        ones_row = jnp.where(lax.broadcasted_iota(jnp.int32, (V_AUG - MB_HEAD_DIM, MB_BLOCK), 0) == 0, 1.0, 0.0)
        for h in heads:
            km_t = jnp.zeros((MB_HEAD_DIM, LANES), _F32)
            for j in range(n_blk):
                kt_blk = kt_ref[0, 0, h, :, j * MB_BLOCK:(j + 1) * MB_BLOCK]
                vt_blk = vt_ref[0, 0, h, :, j * MB_BLOCK:(j + 1) * MB_BLOCK]
                kt_aug_scr[h, j] = jnp.concatenate([kt_blk, jnp.where(ext_row == j, 1.0, 0.0)], axis=0).astype(_BF)
                v_aug_scr[h, j] = jnp.concatenate([vt_blk, ones_row], axis=0).T.astype(_BF)
                mean = jnp.sum(kt_blk, axis=1, keepdims=True) * (1.0 / MB_BLOCK)
                km_t = jnp.where(lane == j, mean, km_t)
            km_scr[h] = km_t.T[:n_blk]
            kt_aug_scr[h, filler] = jnp.concatenate(
                [jnp.zeros((MB_HEAD_DIM, MB_BLOCK), _F32), jnp.where(ext_row == filler, 1.0, 0.0)],
                axis=0).astype(_BF)
            v_aug_scr[h, filler] = jnp.zeros((MB_BLOCK, V_AUG), _BF)

    blk_row = lax.broadcasted_iota(jnp.int32, (n_blk, MB_BLOCK), 0)
    row8 = lax.broadcasted_iota(jnp.int32, (SUBLANES, MB_BLOCK), 0)
    q_aug = []
    for h in heads:
        q_t = qt_ref[0, h]
        gate = _dot(km_scr[h], q_t, precision=_HIGHEST)
        eligible = blk_row < i
        sel, _ = _top3_select(jnp.where(eligible, gate, NEG), 0)
        sel = jnp.where(jnp.logical_or(blk_row == i, eligible), jnp.where(blk_row == i, 1.0, sel), 0.0)
        bias = (sel - 1.0) * (-NEG)
        q_s = q_t * (MB_HEAD_DIM ** -0.5 * LOG2_E)
        tail = jnp.where(row8 == 0, NEG, 0.0)
        pad = jnp.zeros((ext_rows - n_blk - SUBLANES, MB_BLOCK), _F32)
        q_aug_t = jnp.concatenate([q_s, bias, tail, pad], axis=0)
        q_aug.append(q_aug_t.T.astype(_BF))

    n_pairs = (i + ATTN_BLOCKS_PER_TRIP - 1) // ATTN_BLOCKS_PER_TRIP

    def pair_blocks(jj):
        j0 = ATTN_BLOCKS_PER_TRIP * jj
        return (j0,) + tuple(jnp.where(j0 + r < i, j0 + r, filler) for r in range(1, ATTN_BLOCKS_PER_TRIP))

    def fold_max(s):
        return jnp.maximum(s[:, :LANES], s[:, LANES:])

    query_ge_key = (lax.broadcasted_iota(jnp.int32, (MB_BLOCK, MB_BLOCK), 0)
                    >= lax.broadcasted_iota(jnp.int32, (MB_BLOCK, MB_BLOCK), 1))
    m_init = []
    for h in heads:
        s = jnp.where(query_ge_key, _dot(q_aug[h], kt_aug_scr[h, i]), NEG)
        s_scr[h, i] = s
        m_init.append(fold_max(s))

    def score_body(jj, m):
        m = list(m)
        for j in pair_blocks(jj):
            for h in heads:
                s = _dot(q_aug[h], kt_aug_scr[h, j])
                s_scr[h, j] = s
                m[h] = jnp.maximum(m[h], fold_max(s))
        return tuple(m)

    m = lax.fori_loop(0, n_pairs, score_body, tuple(m_init))
    m_b = [jnp.broadcast_to(jnp.max(m[h], axis=1, keepdims=True), (MB_BLOCK, LANES)) for h in heads]

    def weighted_values(h, j):
        s = s_scr[h, j]
        p = jnp.concatenate([jnp.exp2(s[:, :LANES] - m_b[h]), jnp.exp2(s[:, LANES:] - m_b[h])], axis=1)
        return _dot(p.astype(_BF), v_aug_scr[h, j])

    def acc_body(jj, acc):
        acc = list(acc)
        for j in pair_blocks(jj):
            for h in heads:
                acc[h] = acc[h] + weighted_values(h, j)
        return tuple(acc)

    acc = lax.fori_loop(0, n_pairs, acc_body, tuple(weighted_values(h, i) for h in heads))
    o_ref[...] = jnp.concatenate(
        [acc[h][:, :MB_HEAD_DIM] / acc[h][:, MB_HEAD_DIM:MB_HEAD_DIM + 1] for h in heads], axis=1)


def _mb_attn(q_t, k_t_all, v_t_all, layer, n_batch, t):
    n_blk = t // MB_BLOCK
    assert n_blk % SUBLANES == 0 and MB_HEAD_DIM + n_blk + SUBLANES <= AUG_DEPTH
    n_hp = MB_HEADS // ATTN_HEADS
    kern = functools.partial(_mb_attn_kernel, n_blk=n_blk)
    kv_spec = pl.BlockSpec((1, 1, ATTN_HEADS, MB_HEAD_DIM, t), lambda b, hp, i: (b, layer, hp, 0, 0))
    return pl.pallas_call(
        kern,
        grid=(n_batch, n_hp, n_blk),
        in_specs=[pl.BlockSpec((1, ATTN_HEADS, MB_HEAD_DIM, MB_BLOCK), lambda b, hp, i: (b, hp, 0, i)),
                  kv_spec, kv_spec],
        out_specs=pl.BlockSpec((MB_BLOCK, ATTN_HEADS * MB_HEAD_DIM), lambda b, hp, i: (b * n_blk + i, hp)),
        out_shape=jax.ShapeDtypeStruct((n_batch * t, MB_WIDTH), _F32),
        scratch_shapes=[pltpu.VMEM((ATTN_HEADS, n_blk + 1, AUG_DEPTH, MB_BLOCK), _BF),
                        pltpu.VMEM((ATTN_HEADS, n_blk + 1, MB_BLOCK, V_AUG), _BF),
                        pltpu.VMEM((ATTN_HEADS, n_blk, MB_HEAD_DIM), _F32),
                        pltpu.VMEM((ATTN_HEADS, n_blk + 1, MB_BLOCK, MB_BLOCK), _F32)],
        compiler_params=_cparams(("parallel", "parallel", "arbitrary")),
        name="mb_attn",
    )(q_t, k_t_all, v_t_all)


def _kmean_kernel(pt_ref, *refs):
    page_refs, out_ref = refs[:KMEAN_PAGES_PER_STEP], refs[KMEAN_PAGES_PER_STEP]
    for jb in range(KMEAN_PAGES_PER_STEP // PAGES_PER_BLOCK):
        tot = None
        for p in range(PAGES_PER_BLOCK):
            page = page_refs[jb * PAGES_PER_BLOCK + p][0, 0]
            tot = page if tot is None else tot + page
        out_ref[0, 0, 0, :, :, jb:jb + 1] = jnp.sum(tot, axis=-1, keepdims=True) * (1.0 / MB_BLOCK)


def _kmean(cache_kt, pt_flat, n_layers, n_batch, n_pages):
    n_steps = n_pages // KMEAN_PAGES_PER_STEP
    blocks_per_step = KMEAN_PAGES_PER_STEP // PAGES_PER_BLOCK

    def page_spec(p):
        return pl.BlockSpec((1, 1, MB_HEADS, MB_HEAD_DIM, PAGE_SIZE),
                            lambda l, b, c, pt, p=p: (pt[b * n_pages + c * KMEAN_PAGES_PER_STEP + p], l, 0, 0, 0))

    grid_spec = pltpu.PrefetchScalarGridSpec(
        num_scalar_prefetch=1,
        grid=(n_layers, n_batch, n_steps),
        in_specs=[page_spec(p) for p in range(KMEAN_PAGES_PER_STEP)],
        out_specs=pl.BlockSpec((1, 1, 1, MB_HEADS, MB_HEAD_DIM, blocks_per_step),
                               lambda l, b, c, pt: (l, b, c, 0, 0, 0)),
    )
    return pl.pallas_call(
        _kmean_kernel,
        grid_spec=grid_spec,
        out_shape=jax.ShapeDtypeStruct((n_layers, n_batch, n_steps, MB_HEADS, MB_HEAD_DIM, blocks_per_step), _F32),
        compiler_params=_cparams(("parallel", "parallel", "parallel")),
        name="kmean",
    )(pt_flat, *([cache_kt] * KMEAN_PAGES_PER_STEP))


GATE_PAGE_LANE0 = SUBLANES


def _gate_kernel(q_ref, km_ref, pt_ref, idx_ref, *, own):
    n_past = km_ref.shape[3]
    n_pages = pt_ref.shape[2]
    col = lax.broadcasted_iota(jnp.int32, (SAMPLE_T_PAD, n_past), 1)
    lane = lax.broadcasted_iota(jnp.int32, (SAMPLE_T_PAD, LANES), 1)
    page_col = lax.broadcasted_iota(jnp.int32, (SAMPLE_T_PAD, n_pages), 1).astype(_F32)
    pt_row = pt_ref[0].astype(_F32)
    for h in range(MB_HEADS):
        gate = _dot(q_ref[0, h], km_ref[0, h], precision=_HIGHEST)
        _, picks = _top3_select(jnp.where(col < own, gate, NEG), 1)
        out = jnp.zeros((SAMPLE_T_PAD, LANES), jnp.int32)
        for r, first in enumerate(picks):
            out = jnp.where(lane == r, first.astype(jnp.int32), out)
            for p in range(PAGES_PER_BLOCK):
                logical = first * PAGES_PER_BLOCK + p
                phys = jnp.sum(jnp.where(page_col == logical, pt_row, 0.0), axis=1, keepdims=True)
                out = jnp.where(lane == GATE_PAGE_LANE0 + r * PAGES_PER_BLOCK + p, phys.astype(jnp.int32), out)
        idx_ref[0, h] = out


def _gate(q_s, kmean_l, page_table, own):
    n_batch = q_s.shape[0]
    n_past = kmean_l.shape[3]
    n_pages = page_table.shape[1]
    return pl.pallas_call(
        functools.partial(_gate_kernel, own=own),
        grid=(n_batch,),
        in_specs=[pl.BlockSpec((1, MB_HEADS, SAMPLE_T_PAD, MB_HEAD_DIM), lambda b: (b, 0, 0, 0)),
                  pl.BlockSpec((1, MB_HEADS, MB_HEAD_DIM, n_past), lambda b: (b, 0, 0, 0)),
                  pl.BlockSpec((1, 1, n_pages), lambda b: (b, 0, 0))],
        out_specs=pl.BlockSpec((1, MB_HEADS, SAMPLE_T_PAD, LANES), lambda b: (b, 0, 0, 0)),
        out_shape=jax.ShapeDtypeStruct((n_batch, MB_HEADS, SAMPLE_T_PAD, LANES), jnp.int32),
        compiler_params=_cparams(("parallel",)),
        name="mb_gate",
    )(q_s, kmean_l, page_table.reshape(n_batch, 1, n_pages))


def _sample_attn_kernel(pg_ref, idx_ref, q_ref, kn_ref, vn_ref, ck_ref, cv_ref, o_ref, kbuf, vbuf, sem,
                        *, t_s, own, layer):
    n_slots = t_s * MB_TOPK * PAGES_PER_BLOCK
    b = pl.program_id(0)
    row = lax.broadcasted_iota(jnp.int32, (SAMPLE_T_PAD, 1), 0)

    def page_copies(h, par):
        base = (b * MB_HEADS + h) * n_slots
        out = []
        for n in range(n_slots):
            page = pg_ref[base + n]
            out.append(pltpu.make_async_copy(ck_ref.at[page, layer, h], kbuf.at[par, n], sem.at[0, par]))
            out.append(pltpu.make_async_copy(cv_ref.at[page, layer, h], vbuf.at[par, n], sem.at[1, par]))
        return out

    for cp in page_copies(0, 0):
        cp.start()

    def head_body(h, carry):
        par = h % 2

        @pl.when(h + 1 < MB_HEADS)
        def _():
            for cp in page_copies(h + 1, 1 - par):
                cp.start()

        for cp in page_copies(h, par):
            cp.wait()

        q = q_ref[0, h] * (MB_HEAD_DIM ** -0.5)
        q_bf = q.astype(_BF)
        scores = []
        for t in range(t_s):
            for j in range(MB_TOPK):
                blk = idx_ref[((b * MB_HEADS + h) * t_s + t) * MB_TOPK + j]
                allowed = row == jnp.where(blk < own, t, -1)
                for p in range(PAGES_PER_BLOCK):
                    slot = (t * MB_TOPK + j) * PAGES_PER_BLOCK + p
                    s = _dot(q_bf, kbuf[par, slot].astype(_BF))
                    scores.append(jnp.where(allowed, s, NEG))
        kn = kn_ref[0, h]
        vn = vn_ref[0, h]
        own_scores = []
        for c in range(t_s):
            s = jnp.sum(q * kn[c:c + 1, :], axis=1, keepdims=True)
            own_scores.append(jnp.where(row >= c, s, NEG))

        m = own_scores[0]
        for s in own_scores[1:]:
            m = jnp.maximum(m, s)
        for s in scores:
            m = jnp.maximum(m, jnp.max(s, axis=1, keepdims=True))
        l = jnp.zeros((SAMPLE_T_PAD, 1), _F32)
        acc = jnp.zeros((SAMPLE_T_PAD, MB_HEAD_DIM), _F32)
        for c, s in enumerate(own_scores):
            p = jnp.exp(s - m)
            l = l + p
            acc = acc + p * vn[c:c + 1, :]
        for slot, s in enumerate(scores):
            p = jnp.exp(s - m)
            l = l + jnp.sum(p, axis=1, keepdims=True)
            acc = acc + _dot_nt(p.astype(_BF), vbuf[par, slot].astype(_BF))
        o_ref[0, h] = acc / l
        return carry

    lax.fori_loop(0, MB_HEADS, head_body, 0)


def _sample_attn(q_s, k_s, v_s, cache_kt, cache_vt, pages_flat, idx_flat, layer, t_s, own):
    n_batch = q_s.shape[0]
    n_slots = t_s * MB_TOPK * PAGES_PER_BLOCK
    new_spec = pl.BlockSpec((1, MB_HEADS, SAMPLE_T_PAD, MB_HEAD_DIM), lambda b, pg, idx: (b, 0, 0, 0))
    hbm = pl.BlockSpec(memory_space=pl.ANY)
    grid_spec = pltpu.PrefetchScalarGridSpec(
        num_scalar_prefetch=2,
        grid=(n_batch,),
        in_specs=[new_spec, new_spec, new_spec, hbm, hbm],
        out_specs=new_spec,
        scratch_shapes=[pltpu.VMEM((2, n_slots, MB_HEAD_DIM, PAGE_SIZE), _F32),
                        pltpu.VMEM((2, n_slots, MB_HEAD_DIM, PAGE_SIZE), _F32),
                        pltpu.SemaphoreType.DMA((2, 2))],
    )
    return pl.pallas_call(
        functools.partial(_sample_attn_kernel, t_s=t_s, own=own, layer=layer),
        grid_spec=grid_spec,
        out_shape=jax.ShapeDtypeStruct((n_batch, MB_HEADS, SAMPLE_T_PAD, MB_HEAD_DIM), _F32),
        compiler_params=_cparams(("parallel",)),
        name="mb_sample_attn",
    )(pages_flat, idx_flat, q_s, k_s, v_s, cache_kt, cache_vt)


def _pool_kernel(u_ref, hist_ref, w_ref, sc_ref, y_ref, hist_out_ref, ext_scr, *, tm, pos0, t_last):
    ti = pl.program_id(1)

    @pl.when(ti == 0)
    def _():
        ext_scr[0:POOL_HALO, :] = hist_ref[0]

    u = u_ref[...].astype(_F32)
    ext_scr[POOL_HALO:POOL_HALO + tm, :] = u
    pos = pos0 + ti * tm + lax.broadcasted_iota(jnp.int32, (tm, 1), 0)
    for g, w in enumerate(POOL_WINDOWS):
        sl = slice(g * POOL_GROUP_DIM, (g + 1) * POOL_GROUP_DIM)
        tot = u[:, sl]
        for d in range(1, w):
            tot = tot + ext_scr[POOL_HALO - d:POOL_HALO - d + tm, sl]
        cnt = jnp.minimum(pos + 1, w).astype(_F32)
        diff = tot / cnt - u[:, sl]
        y = _dot(diff.astype(_BF), w_ref[g])
        y_ref[:, sl] = y * sc_ref[:, sl]

    @pl.when(ti == pl.num_programs(1) - 1)
    def _():
        hist_out_ref[0] = ext_scr[t_last:t_last + POOL_HALO, :]

    @pl.when(ti < pl.num_programs(1) - 1)
    def _():
        ext_scr[0:POOL_HALO, :] = ext_scr[tm:tm + POOL_HALO, :]


def _pool(z, hist16, pool_w_bf, pool_scale, n_batch, t_pad, tm, pos0, t_valid):
    n_t = t_pad // tm
    t_last = t_valid - (n_t - 1) * tm
    hist_spec = pl.BlockSpec((1, POOL_HALO, POOL_WIDTH), lambda b, ti: (b, 0, 0))
    kern = functools.partial(_pool_kernel, tm=tm, pos0=pos0, t_last=t_last)
    return pl.pallas_call(
        kern,
        grid=(n_batch, n_t),
        in_specs=[pl.BlockSpec((tm, POOL_WIDTH), lambda b, ti: (b * n_t + ti, COL_UC)),
                  hist_spec,
                  pl.BlockSpec((POOL_GROUPS, POOL_GROUP_DIM, POOL_GROUP_DIM), lambda b, ti: (0, 0, 0)),
                  pl.BlockSpec((1, POOL_WIDTH), lambda b, ti: (0, 0))],
        out_specs=[pl.BlockSpec((tm, POOL_WIDTH), lambda b, ti: (b * n_t + ti, 0)), hist_spec],
        out_shape=[jax.ShapeDtypeStruct((n_batch * t_pad, POOL_WIDTH), _F32),
                   jax.ShapeDtypeStruct((n_batch, POOL_HALO, POOL_WIDTH), _F32)],
        scratch_shapes=[pltpu.VMEM((POOL_HALO + tm, POOL_WIDTH), _F32)],
        compiler_params=_cparams(("parallel", "arbitrary")),
        name="pool",
    )(z, hist16, pool_w_bf, pool_scale.reshape(1, POOL_WIDTH))


def _mix_kernel(x_ref, a_ref, b_ref, c_ref, g0_ref, g1_ref, g2_ref, wb_ref, wo_ref, gn_ref, w1_ref, w2_ref, o_ref):
    merged = None
    for br_ref, g_ref, n in ((a_ref, g0_ref, 0), (b_ref, g1_ref, 1), (c_ref, g2_ref, 2)):
        proj = _dot(br_ref[...].astype(_BF), wb_ref[n])
        term = _sigmoid(g_ref[...].astype(_F32)) * proj
        merged = term if merged is None else merged + term
    x = x_ref[...] + _dot(merged.astype(_BF), wo_ref[...])
    ms = jnp.mean(x * x, axis=-1, keepdims=True)
    h = (x * lax.rsqrt(ms + EPS) * gn_ref[...]).astype(_BF)
    hid = jnp.maximum(_dot(h, w1_ref[...]), 0.0)
    hid = (hid * hid).astype(_BF)
    o_ref[...] = x + _dot(hid, w2_ref[...])


def _mix(x, a, b, c, z, wb_bf, wo_bf, g_ffn, w1_bf, w2_bf, tm):
    n = x.shape[0]
    row = lambda w: pl.BlockSpec((tm, w), lambda i: (i, 0))
    gate = lambda k: pl.BlockSpec((tm, D_MODEL), lambda i, k=k: (i, COL_GATE_1024 + k))
    const = lambda shape: pl.BlockSpec(shape, lambda i: (0,) * len(shape), pipeline_mode=pl.Buffered(1))
    return pl.pallas_call(
        _mix_kernel,
        grid=(n // tm,),
        in_specs=[row(D_MODEL), row(BRANCH_WIDTH), row(BRANCH_WIDTH), row(BRANCH_WIDTH),
                  gate(0), gate(1), gate(2),
                  const((N_BRANCH, BRANCH_WIDTH, D_MODEL)), const((D_MODEL, D_MODEL)),
                  const((1, D_MODEL)), const((D_MODEL, D_FF)), const((D_FF, D_MODEL))],
        out_specs=row(D_MODEL),
        out_shape=jax.ShapeDtypeStruct((n, D_MODEL), _F32),
        compiler_params=_cparams(("parallel",)),
        name="mix",
    )(x, a, b, c, z, z, z, wb_bf, wo_bf, g_ffn.reshape(1, D_MODEL), w1_bf, w2_bf)


def _rope_tables(pos):
    half = MB_HEAD_DIM // 2
    inv = ROPE_THETA ** (-jnp.arange(half, dtype=_F32) / half)
    ang = pos.astype(_F32)[:, None] * inv[None, :]
    cos, sin = jnp.cos(ang), jnp.sin(ang)
    cos_t = jnp.concatenate([cos, cos, cos, cos], axis=1)
    sin_t = jnp.concatenate([-sin, sin, -sin, sin], axis=1)
    return cos_t, sin_t


def _tile_rows(n, cap):
    t = cap
    while n % t:
        t //= 2
    return t


def _layer(x, *, n_batch, t_pad, t_valid, pos0, chunk, sub, lb, s0_t, hist16, moba, w):
    n = x.shape[0]
    z_dtype = _BF if t_pad % (2 * SUBLANES) == 0 else _F32
    z = _inproj(x, w["norm_mix"], w["w_in"], _tile_rows(n, 1024), D_IN // 2, z_dtype)
    a_out, s_new_t = _hgrn(z, lb, w["hg_onorm"], s0_t, n_batch, t_pad, chunk, sub, t_valid)
    b_out, kv = moba(z)
    c_out, hist_new = _pool(z, hist16, w["pool_w"], w["pool_scale"], n_batch, t_pad, _tile_rows(t_pad, 512),
                            pos0, t_valid)
    x2 = _mix(x, a_out, b_out, c_out, z, w["w_branch"], w["w_out"], w["norm_ffn"], w["w_ff1"], w["w_ff2"],
              _tile_rows(n, 256))
    return x2, kv, s_new_t, hist_new


def kernel(x_prompt, x_sample, cache_k, cache_v, state_hgrn, state_pool, page_table, norm_mix, w_in, hg_lb,
           hg_onorm, mb_qnorm, mb_knorm, pool_w, pool_scale, w_branch, w_out, norm_ffn, w_ff1, w_ff2):
    n_b, t_p, _ = x_prompt.shape
    n_db, t_s, _ = x_sample.shape
    depth = w_in.shape[0]
    n_pages = page_table.shape[1]
    past_len = n_pages * PAGE_SIZE
    assert t_p % MB_BLOCK == 0 and t_p % HG_CHUNK == 0
    assert past_len % MB_BLOCK == 0 and t_s <= SAMPLE_T_PAD and n_pages % KMEAN_PAGES_PER_STEP == 0
    own_s = past_len // MB_BLOCK

    lb_soft = jax.nn.softmax(hg_lb.astype(_F32), axis=0)
    lb_all = jnp.cumsum(lb_soft, axis=0) - lb_soft[0:1]

    seg = (jnp.arange(MB_WIDTH)[:, None] // MB_HEAD_DIM == jnp.arange(MB_WIDTH)[None, :] // MB_HEAD_DIM).astype(_BF)
    cos_p, sin_p = _rope_tables(jnp.arange(t_p, dtype=jnp.int32))
    cos_s, sin_s = _rope_tables(past_len + jnp.arange(SAMPLE_T_PAD, dtype=jnp.int32))

    cache_kt = jnp.swapaxes(cache_k, 3, 4)
    cache_vt = jnp.swapaxes(cache_v, 3, 4)
    pt_flat = page_table.reshape(-1).astype(jnp.int32)
    kmean_all = _kmean(cache_kt, pt_flat, depth, n_db, n_pages)
    kmean_all = jnp.transpose(kmean_all, (0, 1, 3, 4, 2, 5)).reshape(depth, n_db, MB_HEADS, MB_HEAD_DIM, own_s)

    xp = x_prompt.reshape(n_b * t_p, D_MODEL)
    xs = jnp.pad(x_sample, ((0, 0), (0, SAMPLE_T_PAD - t_s), (0, 0))).reshape(n_db * SAMPLE_T_PAD, D_MODEL)

    s0_p = jnp.zeros((n_b, HG_HEADS, HG_DV, HG_DK), _F32)
    hist_p = jnp.zeros((n_b, POOL_HALO, POOL_WIDTH), _F32)

    kv_p = None
    ks_l, vs_l, sp_l, ss_l, hp_l, hs_l = [], [], [], [], [], []
    for l in range(depth):
        w = dict(norm_mix=norm_mix[l], w_in=w_in[l].astype(_BF), hg_onorm=hg_onorm[l], mb_qnorm=mb_qnorm[l],
                 mb_knorm=mb_knorm[l], pool_w=pool_w[l].astype(_BF), pool_scale=pool_scale[l],
                 w_branch=w_branch[l].astype(_BF), w_out=w_out[l].astype(_BF), norm_ffn=norm_ffn[l],
                 w_ff1=w_ff1[l].astype(_BF), w_ff2=w_ff2[l].astype(_BF))

        def moba_p(z, l=l, w=w, kv_prev=kv_p):
            q_t, k_t_all, v_t_all = _mb_prep_t(z, cos_p, sin_p, w["mb_qnorm"], w["mb_knorm"], seg, n_b, t_p,
                                               _tile_rows(t_p, 512), l, depth, kv_prev)
            return _mb_attn(q_t, k_t_all, v_t_all, l, n_b, t_p), (k_t_all, v_t_all)

        xp, kv_p, sp_t, hp = _layer(
            xp, n_batch=n_b, t_pad=t_p, t_valid=t_p, pos0=0, chunk=HG_CHUNK, sub=HG_SUB, lb=lb_all[l],
            s0_t=s0_p, hist16=hist_p, moba=moba_p, w=w)

        def moba_s(z, l=l, w=w):
            q, k, v = _mb_prep(z, cos_s, sin_s, w["mb_qnorm"], w["mb_knorm"], seg, n_db, SAMPLE_T_PAD, SAMPLE_T_PAD)
            idx = _gate(q, kmean_all[l], page_table, own_s)
            idx_flat = idx[:, :, :t_s, :MB_TOPK].reshape(-1)
            pages_flat = idx[:, :, :t_s, GATE_PAGE_LANE0:GATE_PAGE_LANE0 + MB_TOPK * PAGES_PER_BLOCK].reshape(-1)
            o = _sample_attn(q, k, v, cache_kt, cache_vt, pages_flat, idx_flat, l, t_s, own_s)
            return jnp.transpose(o, (0, 2, 1, 3)).reshape(n_db * SAMPLE_T_PAD, MB_WIDTH), (k, v)

        s0_s = jnp.swapaxes(state_hgrn[l], -1, -2)
        hist_s = jnp.pad(state_pool[l], ((0, 0), (POOL_HALO - POOL_HIST, 0), (0, 0)))
        xs, (ks, vs), ss_t, hs = _layer(
            xs, n_batch=n_db, t_pad=SAMPLE_T_PAD, t_valid=t_s, pos0=past_len, chunk=SAMPLE_T_PAD,
            sub=SAMPLE_T_PAD, lb=lb_all[l], s0_t=s0_s, hist16=hist_s, moba=moba_s, w=w)

        ks_l.append(ks[:, :, :t_s]); vs_l.append(vs[:, :, :t_s])
        sp_l.append(jnp.swapaxes(sp_t, -1, -2)); ss_l.append(jnp.swapaxes(ss_t, -1, -2))
        hp_l.append(hp[:, POOL_HALO - POOL_HIST:]); hs_l.append(hs[:, POOL_HALO - POOL_HIST:])

    y_p = xp.reshape(n_b, t_p, D_MODEL)
    y_s = xs.reshape(n_db, SAMPLE_T_PAD, D_MODEL)[:, :t_s]
    k_p = jnp.swapaxes(kv_p[0], 3, 4)
    v_p = jnp.swapaxes(kv_p[1], 3, 4)
    return (y_p, y_s, k_p, v_p, jnp.stack(ks_l, axis=1), jnp.stack(vs_l, axis=1),
            jnp.stack(sp_l, axis=0), jnp.stack(ss_l, axis=0), jnp.stack(hp_l, axis=0), jnp.stack(hs_l, axis=0))
```
